```python
import jax, jax.numpy as jnp
from jax import lax
import numpy as np

D_MODEL = 4096
BATCH = 4
SEQ = 2048
DEPTH = 4
DEC_BATCH = 128
DEC_SEQ = 4
PAST_LEN = 16384
PAGE_SIZE = 128

D_MIX = D_MODEL
POOL_W = D_MIX // 4
POOL_WINDOWS = (2, 4, 8, 16)
POOL_GROUP = POOL_W // len(POOL_WINDOWS)
POOL_BUF = max(POOL_WINDOWS) - 1
RWKV_W = D_MIX - POOL_W
HEAD_SIZE = 64
N_HEADS = RWKV_W // HEAD_SIZE
W_LORA = max(32, int(round(1.8 * RWKV_W ** 0.5 / 32)) * 32)
A_LORA = max(32, int(round(1.8 * RWKV_W ** 0.5 / 32)) * 32)
V_LORA = max(32, int(round(1.3 * RWKV_W ** 0.5 / 32)) * 32)
G_LORA = max(32, int(round(0.6 * RWKV_W ** 0.8 / 32)) * 32)
C_MAIN = POOL_W + 3 * RWKV_W + W_LORA + A_LORA + G_LORA
N_GROUPS = 4
EXP_PER_GROUP = 8
N_EXPERTS = N_GROUPS * EXP_PER_GROUP
TOP_K = 2
D_EXPERT = D_MODEL // 8
N_MOD = 6
RMS_EPS = 1e-6
GN_EPS = 64e-5

kernel_name = 'hymba_pool_rwkv7_hmoe_adaln_step'


def rms_norm(x, g):
    x32 = x.astype(jnp.float32)
    y = x32 * lax.rsqrt(jnp.mean(x32 * x32, axis=-1, keepdims=True) + RMS_EPS)
    return (y * g.astype(jnp.float32)).astype(x.dtype)


def modulate(h, shift, scale):
    return (h * (1 + scale[:, None, :]) + shift[:, None, :]).astype(h.dtype)


def pool_mixer(u, u_past, start_pos, w_pool, pool_scale):
    f32 = jnp.float32
    B, T, _ = u.shape
    L = POOL_BUF
    ext = jnp.concatenate([u_past.astype(u.dtype), u], axis=1)
    ext32 = ext.astype(f32)
    cs = jnp.concatenate([jnp.zeros((B, 1, POOL_W), f32), jnp.cumsum(ext32, axis=1)], axis=1)
    pos = start_pos + jnp.arange(T)
    diffs = []
    for gi, win in enumerate(POOL_WINDOWS):
        sl = slice(gi * POOL_GROUP, (gi + 1) * POOL_GROUP)
        win_sum = cs[:, L + 1:L + 1 + T, sl] - cs[:, L + 1 - win:L + 1 - win + T, sl]
        cnt = jnp.minimum(win, pos + 1).astype(f32)[None, :, None]
        diffs.append(win_sum / cnt - ext32[:, L:, sl])
    d = jnp.stack(diffs, axis=2).astype(u.dtype)
    out = jnp.einsum('btgc,gcd->btgd', d, w_pool).reshape(B, T, POOL_W) * pool_scale
    return out, ext[:, T:, :]


def wkv7(s0, r, decay, k, v, kk, a):
    def step(s, inp):
        r_t, d_t, k_t, v_t, kk_t, a_t = inp
        s_kk = jnp.einsum('bhvk,bhk->bhv', s, kk_t)
        s = (s * d_t[:, :, None, :] - s_kk[..., None] * (kk_t * a_t)[:, :, None, :]
             + v_t[..., None] * k_t[:, :, None, :])
        return s, jnp.einsum('bhvk,bhk->bhv', s, r_t)
    xs = tuple(jnp.swapaxes(t, 0, 1) for t in (r, decay, k, v, kk, a))
    s, o = lax.scan(step, s0, xs)
    return s, jnp.swapaxes(o, 0, 1)


def token_mixer(h, h_prev, pool_past, wkv_past, v_first, start_pos, l, p):
    f32 = jnp.float32
    B, T, _ = h.shape
    w_proj, mu = p['w_in'][l], p['mu'][l]
    sizes = [RWKV_W, RWKV_W, RWKV_W, W_LORA, A_LORA, G_LORA]
    if l > 0:
        w_proj = jnp.concatenate([w_proj, p['w_vres_down'][l - 1]], axis=1)
        mu = jnp.concatenate([mu, p['mu_vres'][l - 1]])
        sizes = sizes + [V_LORA]
    proj = jnp.concatenate([h_prev[:, None, :].astype(h.dtype), h], axis=1) @ w_proj
    cur, prev = proj[:, 1:], proj[:, :-1]
    pool_out, new_pool = pool_mixer(cur[..., :POOL_W], pool_past, start_pos,
                                    p['w_pool'][l], p['pool_scale'][l])
    z = cur[..., POOL_W:]
    z = z + (prev[..., POOL_W:] - z) * mu
    parts = jnp.split(z, list(np.cumsum(sizes)[:-1]), axis=-1)
    r, k, v, wd, ad, gd = parts[:6]
    w_log = -jax.nn.softplus(-(p['w_decay0'][l] + jnp.tanh(wd) @ p['w_decay2'][l]).astype(f32)) - 0.5
    decay = jnp.exp(-jnp.exp(w_log))
    a = jax.nn.sigmoid((p['w_a0'][l] + ad @ p['w_a2'][l]).astype(f32))
    g = jax.nn.sigmoid(gd) @ p['w_g2'][l]
    if l == 0:
        v_first = v
    else:
        vd = parts[6]
        v = v + (v_first - v) * jax.nn.sigmoid(p['w_v0'][l - 1] + vd @ p['w_v2'][l - 1])
    heads = lambda t: t.astype(f32).reshape(B, T, N_HEADS, HEAD_SIZE)
    kk = heads(k * p['k_k'][l])
    kk = kk / jnp.maximum(jnp.sqrt(jnp.sum(kk * kk, axis=-1, keepdims=True)), 1e-12)
    k_mod = heads(k.astype(f32) * (1 + (a - 1) * p['k_a'][l].astype(f32)))
    r_h, v_h, a_h = heads(r), heads(v), heads(a)
    s_new, o = wkv7(wkv_past.astype(f32), r_h, heads(decay), k_mod, v_h, kk, a_h)
    mean = jnp.mean(o, axis=-1, keepdims=True)
    var = jnp.mean(jnp.square(o - mean), axis=-1, keepdims=True)
    o = ((o - mean) * lax.rsqrt(var + GN_EPS)).reshape(B, T, RWKV_W)
    o = o * p['ln_x_w'][l].astype(f32) + p['ln_x_b'][l].astype(f32)
    bonus = jnp.sum(r_h * k_mod * p['r_k'][l].astype(f32), axis=-1, keepdims=True) * v_h
    o = (o + bonus.reshape(B, T, RWKV_W)) * g.astype(f32)
    mixed = jnp.concatenate([pool_out, o.astype(h.dtype)], axis=-1) @ p['w_out'][l]
    return mixed, v_first, h[:, -1], new_pool, s_new.astype(wkv_past.dtype)


def hier_moe(h, l, p):
    f32 = jnp.float32
    B, T, D = h.shape
    t = h.reshape(B * T, D)
    p_group = jax.nn.softmax((t @ p['w_route_group'][l] + p['b_route_group'][l]).astype(f32), axis=-1)
    p_top, g_idx = lax.top_k(p_group, 1)
    le = (t @ p['w_route_expert'][l] + p['b_route_expert'][l]).astype(f32)
    le = le.reshape(-1, N_GROUPS, EXP_PER_GROUP)
    le = jnp.take_along_axis(le, g_idx[:, :, None], axis=1)[:, 0]
    v_top, e_idx = lax.top_k(le, TOP_K)
    wts = p_top * jax.nn.softmax(v_top, axis=-1)
    gidx = g_idx * EXP_PER_GROUP + e_idx
    comb = jnp.einsum('nk,nke->ne', wts, jax.nn.one_hot(gidx, N_EXPERTS, dtype=f32)).astype(h.dtype)
    out = jnp.zeros_like(t)
    for e in range(N_EXPERTS):
        hid = jax.nn.silu(t @ p['w_exp_gate'][l, e]) * (t @ p['w_exp_up'][l, e])
        out = out + comb[:, e:e + 1] * (hid @ p['w_exp_down'][l, e])
    return out.reshape(B, T, D)


def trunk(x, c, shift_st, pool_st, wkv_st, start_pos, p):
    mod = (jax.nn.silu(c) @ p['w_ada']).reshape(c.shape[0], N_MOD, D_MODEL)
    v_first = None
    new_shift, new_pool, new_wkv = [], [], []
    for l in range(DEPTH):
        m = mod + p['b_ada'][l]
        h = modulate(rms_norm(x, p['g_mix'][l]), m[:, 0], m[:, 1])
        mixed, v_first, hs, pb, sw = token_mixer(h, shift_st[l], pool_st[l], wkv_st[l],
                                                 v_first, start_pos, l, p)
        x = (x + m[:, 2, None, :] * mixed).astype(x.dtype)
        h2 = modulate(rms_norm(x, p['g_ffn'][l]), m[:, 3], m[:, 4])
        x = (x + m[:, 5, None, :] * hier_moe(h2, l, p)).astype(x.dtype)
        new_shift.append(hs.astype(shift_st.dtype))
        new_pool.append(pb.astype(pool_st.dtype))
        new_wkv.append(sw)
    y = rms_norm(x, p['g_final'])
    return y, jnp.stack(new_shift), jnp.stack(new_pool), jnp.stack(new_wkv)


def setup_inputs(seed: int = 0) -> dict:
    key = jax.random.key(seed)
    ks = iter(jax.random.split(key, 40))
    nrm = lambda shape, scale: scale * jax.random.normal(next(ks), shape, jnp.float32)
    uni = lambda shape: jax.random.uniform(next(ks), shape, jnp.float32)
    L1 = DEPTH - 1
    return {
        'x_prompt': nrm((BATCH, SEQ, D_MODEL), 1.0),
        'x_sample': nrm((DEC_BATCH, DEC_SEQ, D_MODEL), 1.0),
        'state_shift': nrm((DEPTH, DEC_BATCH, D_MODEL), 1.0),
        'state_pool': nrm((DEPTH, DEC_BATCH, POOL_BUF, POOL_W), 1.0),
        'state_wkv': nrm((DEPTH, DEC_BATCH, N_HEADS, HEAD_SIZE, HEAD_SIZE), 0.5),
        'c_prompt': nrm((BATCH, D_MODEL), 1.0),
        'c_sample': nrm((DEC_BATCH, D_MODEL), 1.0),
        'w_ada': nrm((D_MODEL, N_MOD * D_MODEL), 0.5 * D_MODEL ** -0.5),
        'b_ada': nrm((DEPTH, N_MOD, D_MODEL), 0.1),
        'g_mix': 1.0 + nrm((DEPTH, D_MODEL), 0.05),
        'g_ffn': 1.0 + nrm((DEPTH, D_MODEL), 0.05),
        'g_final': 1.0 + nrm((D_MODEL,), 0.05),
        'w_in': nrm((DEPTH, D_MODEL, C_MAIN), D_MODEL ** -0.5),
        'w_vres_down': nrm((L1, D_MODEL, V_LORA), D_MODEL ** -0.5),
        'mu': uni((DEPTH, C_MAIN - POOL_W)),
        'mu_vres': uni((L1, V_LORA)),
        'w_pool': nrm((DEPTH, len(POOL_WINDOWS), POOL_GROUP, POOL_GROUP), POOL_GROUP ** -0.5),
        'pool_scale': 1.0 + nrm((DEPTH, POOL_W), 0.1),
        'w_decay0': -1.0 + nrm((DEPTH, RWKV_W), 0.5),
        'w_decay2': nrm((DEPTH, W_LORA, RWKV_W), 0.5 * W_LORA ** -0.5),
        'w_a0': nrm((DEPTH, RWKV_W), 0.2),
        'w_a2': nrm((DEPTH, A_LORA, RWKV_W), 0.5 * A_LORA ** -0.5),
        'w_g2': nrm((DEPTH, G_LORA, RWKV_W), G_LORA ** -0.5),
        'w_v0': nrm((L1, RWKV_W), 0.2),
        'w_v2': nrm((L1, V_LORA, RWKV_W), 0.5 * V_LORA ** -0.5),
        'k_k': 0.85 + nrm((DEPTH, RWKV_W), 0.05),
        'k_a': 1.0 + nrm((DEPTH, RWKV_W), 0.05),
        'r_k': nrm((DEPTH, N_HEADS, HEAD_SIZE), 0.1),
        'ln_x_w': 1.0 + nrm((DEPTH, RWKV_W), 0.05),
        'ln_x_b': nrm((DEPTH, RWKV_W), 0.02),
        'w_out': nrm((DEPTH, D_MIX, D_MODEL), D_MIX ** -0.5),
        'w_route_group': nrm((DEPTH, D_MODEL, N_GROUPS), D_MODEL ** -0.5),
        'b_route_group': nrm((DEPTH, N_GROUPS), 0.01),
        'w_route_expert': nrm((DEPTH, D_MODEL, N_EXPERTS), D_MODEL ** -0.5),
        'b_route_expert': nrm((DEPTH, N_EXPERTS), 0.01),
        'w_exp_gate': nrm((DEPTH, N_EXPERTS, D_MODEL, D_EXPERT), D_MODEL ** -0.5),
        'w_exp_up': nrm((DEPTH, N_EXPERTS, D_MODEL, D_EXPERT), D_MODEL ** -0.5),
        'w_exp_down': nrm((DEPTH, N_EXPERTS, D_EXPERT, D_MODEL), D_EXPERT ** -0.5),
    }


def reference(x_prompt, x_sample, state_shift, state_pool, state_wkv, c_prompt, c_sample,
              w_ada, b_ada, g_mix, g_ffn, g_final, w_in, w_vres_down, mu, mu_vres,
              w_pool, pool_scale, w_decay0, w_decay2, w_a0, w_a2, w_g2, w_v0, w_v2,
              k_k, k_a, r_k, ln_x_w, ln_x_b, w_out, w_route_group, b_route_group,
              w_route_expert, b_route_expert, w_exp_gate, w_exp_up, w_exp_down):
    p = dict(w_ada=w_ada, b_ada=b_ada, g_mix=g_mix, g_ffn=g_ffn, g_final=g_final,
             w_in=w_in, w_vres_down=w_vres_down, mu=mu, mu_vres=mu_vres,
             w_pool=w_pool, pool_scale=pool_scale, w_decay0=w_decay0, w_decay2=w_decay2,
             w_a0=w_a0, w_a2=w_a2, w_g2=w_g2, w_v0=w_v0, w_v2=w_v2,
             k_k=k_k, k_a=k_a, r_k=r_k, ln_x_w=ln_x_w, ln_x_b=ln_x_b, w_out=w_out,
             w_route_group=w_route_group, b_route_group=b_route_group,
             w_route_expert=w_route_expert, b_route_expert=b_route_expert,
             w_exp_gate=w_exp_gate, w_exp_up=w_exp_up, w_exp_down=w_exp_down)
    bp = x_prompt.shape[0]
    shift0 = jnp.zeros((DEPTH, bp, D_MODEL), state_shift.dtype)
    pool0 = jnp.zeros((DEPTH, bp, POOL_BUF, POOL_W), state_pool.dtype)
    wkv0 = jnp.zeros((DEPTH, bp, N_HEADS, HEAD_SIZE, HEAD_SIZE), state_wkv.dtype)
    y_prompt, shift_p, pool_p, wkv_p = trunk(x_prompt, c_prompt, shift0, pool0, wkv0, 0, p)
    y_sample, shift_s, pool_s, wkv_s = trunk(x_sample, c_sample, state_shift, state_pool,
                                             state_wkv, PAST_LEN, p)
    return (y_prompt, y_sample, shift_p, pool_p, wkv_p, shift_s, pool_s, wkv_s)
```

```python
import functools

import jax
import jax.numpy as jnp
from jax import lax
from jax.experimental import pallas as pl
from jax.experimental.pallas import tpu as pltpu

F32 = jnp.float32
BF16 = jnp.bfloat16

HEAD = 64
PAIR = 2 * HEAD
SUBLANES = 8
POOL_WINDOWS = (2, 4, 8, 16)
POOL_BUF = max(POOL_WINDOWS) - 1
EXP_PER_GROUP = 8
TOP_K = 2
RMS_EPS = 1e-6
GN_EPS = 64e-5
MOE_TM = 256
PAST_LEN = 16384


def _mm_kernel(a_ref, b_ref, o_ref, *, highest):
    if highest:
        acc = jnp.dot(a_ref[...], b_ref[...], precision=lax.Precision.HIGHEST,
                      preferred_element_type=F32)
    else:
        acc = jnp.dot(a_ref[...].astype(BF16), b_ref[...].astype(BF16),
                      preferred_element_type=F32)
    o_ref[...] = acc.astype(o_ref.dtype)


def _mm(a, b, *, tm, tn, b_lead=(), n_off=0, n=None, out_dtype=F32, highest=False):
    m, k = a.shape
    assert b.shape[-2] == k
    n = b.shape[-1] if n is None else n
    assert m % tm == 0 and n % tn == 0, (m, tm, n, tn)
    lead = tuple(b_lead)
    return pl.pallas_call(
        functools.partial(_mm_kernel, highest=highest),
        grid=(m // tm, n // tn),
        in_specs=[pl.BlockSpec((tm, k), lambda i, j: (i, 0)),
                  pl.BlockSpec((None,) * len(lead) + (k, tn),
                               lambda i, j: lead + (0, j + n_off))],
        out_specs=pl.BlockSpec((tm, tn), lambda i, j: (i, j)),
        out_shape=jax.ShapeDtypeStruct((m, n), out_dtype),
        compiler_params=pltpu.CompilerParams(dimension_semantics=("parallel", "arbitrary")),
    )(a, b)


def _pool_mm(d, w_pool, layer, *, tm):
    m = d.shape[0]
    g, c = w_pool.shape[1], w_pool.shape[2]
    return pl.pallas_call(
        functools.partial(_mm_kernel, highest=False),
        grid=(m // tm, g),
        in_specs=[pl.BlockSpec((tm, c), lambda i, j: (i, j)),
                  pl.BlockSpec((None, None, c, c), lambda i, j: (layer, j, 0, 0))],
        out_specs=pl.BlockSpec((tm, c), lambda i, j: (i, j)),
        out_shape=jax.ShapeDtypeStruct((m, g * c), F32),
        compiler_params=pltpu.CompilerParams(dimension_semantics=("parallel", "arbitrary")),
    )(d, w_pool)


def _seg_sum(x, ones_blockdiag):
    hi = x.astype(BF16)
    lo = (x - hi.astype(F32)).astype(BF16)
    return (jnp.dot(hi, ones_blockdiag, preferred_element_type=F32)
            + jnp.dot(lo, ones_blockdiag, preferred_element_type=F32))


def _wkv_kernel(r_ref, d_ref, k_ref, v_ref, kk_ref, kka_ref, s0_ref, o_ref, s_ref, *, tc, npairs):
    @pl.when(pl.program_id(1) == 0)
    def _():
        s_ref[...] = s0_ref[...]

    lane = lax.broadcasted_iota(jnp.int32, (HEAD, PAIR), 1)
    row = lax.broadcasted_iota(jnp.int32, (HEAD, PAIR), 0)
    diag = ((lane & (HEAD - 1)) == row).astype(F32)
    jr = lax.broadcasted_iota(jnp.int32, (PAIR, PAIR), 0) >= HEAD
    jc = lax.broadcasted_iota(jnp.int32, (PAIR, PAIR), 1) >= HEAD
    ones_blockdiag = (jr == jc).astype(BF16)

    rows = min(tc, SUBLANES)
    row_id = lax.broadcasted_iota(jnp.int32, (rows, PAIR), 0)

    def row_group(gi, carry):
        base = pl.multiple_of(gi * rows, rows)

        def rowvec(ref, p, i):
            return ref[0, pl.ds(base, rows), p * PAIR:(p + 1) * PAIR][i:i + 1]

        o_tiles = [jnp.zeros((rows, PAIR), F32)] * npairs
        for i in range(rows):
            xs = []
            for p in range(npairs):
                xs.append(s_ref[0, p] * rowvec(kk_ref, p, i))
                xs.append(diag * rowvec(v_ref, p, i))
            y = _seg_sum(jnp.concatenate(xs, axis=0), ones_blockdiag)
            qs = []
            for p in range(npairs):
                s_kk = y[p * PAIR:p * PAIR + HEAD]
                v_col = y[p * PAIR + HEAD:(p + 1) * PAIR]
                s = (s_ref[0, p] * rowvec(d_ref, p, i) - s_kk * rowvec(kka_ref, p, i)
                     + v_col * rowvec(k_ref, p, i))
                s_ref[0, p] = s
                qs.append(s * rowvec(r_ref, p, i))
            z = _seg_sum(jnp.concatenate(qs, axis=0), ones_blockdiag)
            for p in range(npairs):
                o_col = z[p * HEAD:(p + 1) * HEAD]
                o_row = jnp.sum(o_col * diag, axis=0, keepdims=True)
                o_tiles[p] = jnp.where(row_id == i, o_row, o_tiles[p])
        for p in range(npairs):
            o_ref[0, pl.ds(base, rows), p * PAIR:(p + 1) * PAIR] = o_tiles[p]
        return carry

    lax.fori_loop(0, tc // rows, row_group, 0)


def _pair_state(s):
    b, h = s.shape[:2]
    return s.reshape(b, h // 2, 2, HEAD, HEAD).transpose(0, 1, 3, 2, 4).reshape(b, h // 2, HEAD, PAIR)


def _unpair_state(s):
    b, hp = s.shape[:2]
    return s.reshape(b, hp, HEAD, 2, HEAD).transpose(0, 1, 3, 2, 4).reshape(b, 2 * hp, HEAD, HEAD)


def _wkv(r, decay, k, v, kk, kka, s0, *, tc):
    b, t, w = r.shape
    npairs = w // PAIR
    assert t % tc == 0
    seq = pl.BlockSpec((1, tc, w), lambda i, c: (i, c, 0))
    st = pl.BlockSpec((1, npairs, HEAD, PAIR), lambda i, c: (i, 0, 0, 0))
    o, s = pl.pallas_call(
        functools.partial(_wkv_kernel, tc=tc, npairs=npairs),
        grid=(b, t // tc),
        in_specs=[seq] * 6 + [st],
        out_specs=[seq, st],
        out_shape=[jax.ShapeDtypeStruct((b, t, w), F32),
                   jax.ShapeDtypeStruct((b, npairs, HEAD, PAIR), F32)],
        compiler_params=pltpu.CompilerParams(dimension_semantics=("parallel", "arbitrary")),
    )(r, decay, k, v, kk, kka, _pair_state(s0))
    return o, _unpair_state(s)


def _moe_up_kernel(te_ref, tv_ref, x_ref, wg_ref, wu_ref, h_ref):
    i = pl.program_id(0)

    @pl.when(tv_ref[i] > 0)
    def _():
        x = x_ref[...]
        g = jnp.dot(x, wg_ref[...].astype(BF16), preferred_element_type=F32)
        u = jnp.dot(x, wu_ref[...].astype(BF16), preferred_element_type=F32)
        h_ref[...] = (g * jax.nn.sigmoid(g) * u).astype(h_ref.dtype)

    @pl.when(tv_ref[i] == 0)
    def _():
        h_ref[...] = jnp.zeros_like(h_ref)


def _moe_down_kernel(te_ref, tv_ref, h_ref, wd_ref, w_ref, y_ref):
    i = pl.program_id(0)

    @pl.when(tv_ref[i] > 0)
    def _():
        y = jnp.dot(h_ref[...], wd_ref[...].astype(BF16), preferred_element_type=F32)
        y_ref[...] = y * w_ref[...]

    @pl.when(tv_ref[i] == 0)
    def _():
        y_ref[...] = jnp.zeros_like(y_ref)


def _moe_experts(xs, row_w, tile_expert, tile_valid, w_gate, w_up, w_down, layer):
    p, d = xs.shape
    de = w_gate.shape[-1]
    nt = p // MOE_TM
    hid = pl.pallas_call(
        _moe_up_kernel,
        grid_spec=pltpu.PrefetchScalarGridSpec(
            num_scalar_prefetch=2, grid=(nt,),
            in_specs=[pl.BlockSpec((MOE_TM, d), lambda i, te, tv: (i, 0)),
                      pl.BlockSpec((None, None, d, de), lambda i, te, tv: (layer, te[i], 0, 0)),
                      pl.BlockSpec((None, None, d, de), lambda i, te, tv: (layer, te[i], 0, 0))],
            out_specs=pl.BlockSpec((MOE_TM, de), lambda i, te, tv: (i, 0))),
        out_shape=jax.ShapeDtypeStruct((p, de), BF16),
        compiler_params=pltpu.CompilerParams(dimension_semantics=("arbitrary",)),
    )(tile_expert, tile_valid, xs, w_gate, w_up)
    return pl.pallas_call(
        _moe_down_kernel,
        grid_spec=pltpu.PrefetchScalarGridSpec(
            num_scalar_prefetch=2, grid=(nt,),
            in_specs=[pl.BlockSpec((MOE_TM, de), lambda i, te, tv: (i, 0)),
                      pl.BlockSpec((None, None, de, d), lambda i, te, tv: (layer, te[i], 0, 0)),
                      pl.BlockSpec((MOE_TM, 1), lambda i, te, tv: (i, 0))],
            out_specs=pl.BlockSpec((MOE_TM, d), lambda i, te, tv: (i, 0))),
        out_shape=jax.ShapeDtypeStruct((p, d), F32),
        compiler_params=pltpu.CompilerParams(dimension_semantics=("arbitrary",)),
    )(tile_expert, tile_valid, hid, w_down, row_w)


def _hier_moe(h2, layer, w_route, b_route_group, b_route_expert, w_gate, w_up, w_down, *, tm_route):
    n, d = h2.shape
    n_groups = b_route_group.shape[-1]
    n_experts = b_route_expert.shape[-1]
    logits = _mm(h2, w_route, tm=tm_route, tn=w_route.shape[-1], b_lead=(layer,), highest=True)
    p_group = jax.nn.softmax(logits[:, :n_groups] + b_route_group[layer], axis=-1)
    p_top, g_idx = lax.top_k(p_group, 1)
    le = (logits[:, n_groups:n_groups + n_experts] + b_route_expert[layer])
    le = le.reshape(n, n_groups, EXP_PER_GROUP)
    le = jnp.take_along_axis(le, g_idx[:, :, None], axis=1)[:, 0]
    v_top, e_idx = lax.top_k(le, TOP_K)
    wts = p_top * jax.nn.softmax(v_top, axis=-1)
    gidx = (g_idx * EXP_PER_GROUP + e_idx).astype(jnp.int32)

    na = n * TOP_K
    nt = na // MOE_TM + n_experts
    e_flat = gidx.reshape(na)
    order = jnp.argsort(e_flat, stable=True).astype(jnp.int32)
    sorted_e = e_flat[order]
    counts = jnp.zeros((n_experts,), jnp.int32).at[e_flat].add(1)
    tiles_per = (counts + MOE_TM - 1) // MOE_TM
    tile_end = jnp.cumsum(tiles_per)
    pad_start = (tile_end - tiles_per) * MOE_TM
    count_start = jnp.cumsum(counts) - counts
    dest = pad_start[sorted_e] + jnp.arange(na, dtype=jnp.int32) - count_start[sorted_e]
    row_token = jnp.zeros((nt * MOE_TM,), jnp.int32).at[dest].set(order // TOP_K)
    row_w = jnp.zeros((nt * MOE_TM,), F32).at[dest].set(wts.reshape(na)[order])
    pos = jnp.zeros((na,), jnp.int32).at[order].set(dest)
    tile_ids = jnp.arange(nt, dtype=jnp.int32)
    tile_expert = jnp.minimum(jnp.searchsorted(tile_end, tile_ids, side='right'),
                              n_experts - 1).astype(jnp.int32)
    tile_valid = (tile_ids < tile_end[-1]).astype(jnp.int32)

    xs = jnp.take(h2.astype(BF16), row_token, axis=0)
    ys = _moe_experts(xs, row_w[:, None], tile_expert, tile_valid, w_gate, w_up, w_down, layer)
    picked = jnp.take(ys, pos, axis=0).reshape(n, TOP_K, d)
    return picked[:, 0] + picked[:, 1]


def _rms_norm(x, g):
    return x * lax.rsqrt(jnp.mean(x * x, axis=-1, keepdims=True) + RMS_EPS) * g


def _shift_rows(cur, first):
    return jnp.concatenate([first, cur[:, :-1]], axis=1)


def _pool_diffs(u, u_past, start_pos):
    b, t, pw = u.shape
    grp = pw // len(POOL_WINDOWS)
    ext = jnp.concatenate([u_past, u], axis=1)
    pos = start_pos + jnp.arange(t)
    diffs = []
    for gi, win in enumerate(POOL_WINDOWS):
        e = ext[:, :, gi * grp:(gi + 1) * grp]
        win_sum = e[:, POOL_BUF:POOL_BUF + t]
        for i in range(1, win):
            win_sum = win_sum + e[:, POOL_BUF - i:POOL_BUF - i + t]
        cnt = jnp.minimum(win, pos + 1).astype(F32)[None, :, None]
        diffs.append(win_sum / cnt - e[:, POOL_BUF:])
    return jnp.concatenate(diffs, axis=-1), ext[:, t:]


def kernel(x_prompt, x_sample, state_shift, state_pool, state_wkv, c_prompt, c_sample, w_ada, b_ada, g_mix, g_ffn, g_final, w_in, w_vres_down, mu, mu_vres, w_pool, pool_scale, w_decay0, w_decay2, w_a0, w_a2, w_g2, w_v0, w_v2, k_k, k_a, r_k, ln_x_w, ln_x_b, w_out, w_route_group, b_route_group, w_route_expert, b_route_expert, w_exp_gate, w_exp_up, w_exp_down):
    bp, tp, d = x_prompt.shape
    bs, ts, _ = x_sample.shape
    depth = w_in.shape[0]
    n_heads = state_wkv.shape[2]
    rw = n_heads * HEAD
    pw = state_pool.shape[-1]
    n_mod = w_ada.shape[1] // d
    n_p, n_s = bp * tp, bs * ts
    n_tok = n_p + n_s
    c_main = pw + 3 * rw
    n_tail = w_in.shape[-1] - c_main
    v_lora = w_vres_down.shape[-1]
    w_lora, a_lora = w_decay2.shape[1], w_a2.shape[1]
    g_lora = w_g2.shape[1]

    tm_tok = n_tok // 8
    tm_proj = (n_tok + bs) // 8

    def seqs(a):
        return a[:n_p].reshape(bp, tp, -1), a[n_p:n_tok].reshape(bs, ts, -1)

    def stream(a_p, a_s):
        return jnp.concatenate([a_p.reshape(n_p, -1), a_s.reshape(n_s, -1)], axis=0)

    c_all = jnp.concatenate([c_prompt, c_sample], axis=0)
    rows_c = -(-c_all.shape[0] // 16) * 16
    c_act = jnp.pad(jax.nn.silu(c_all), ((0, rows_c - c_all.shape[0]), (0, 0))).astype(BF16)
    mod = _mm(c_act, w_ada, tm=rows_c, tn=512)[:bp + bs].reshape(bp + bs, n_mod, d)

    def per_token(vec):
        return stream(jnp.broadcast_to(vec[:bp, None], (bp, tp, d)),
                      jnp.broadcast_to(vec[bp:, None], (bs, ts, d)))

    w_tail = jnp.concatenate(
        [w_in[:, :, c_main:],
         jnp.concatenate([jnp.zeros((1, d, v_lora), w_in.dtype), w_vres_down], axis=0)], axis=-1)
    mu_tail = jnp.concatenate(
        [mu[:, 3 * rw:], jnp.concatenate([jnp.zeros((1, v_lora), mu.dtype), mu_vres], axis=0)], axis=-1)
    route_pad = 128 - (w_route_group.shape[-1] + w_route_expert.shape[-1])
    w_route = jnp.concatenate(
        [w_route_group, w_route_expert, jnp.zeros((depth, d, route_pad), F32)], axis=-1)

    x = stream(x_prompt, x_sample)
    v_first = None
    shift_p, pool_p, wkv_p, shift_s, pool_s, wkv_s = [], [], [], [], [], []
    zero_state = jnp.zeros((bp, n_heads, HEAD, HEAD), F32)
    for l in range(depth):
        m = mod + b_ada[l]
        h = _rms_norm(x, g_mix[l]) * (1 + per_token(m[:, 1])) + per_token(m[:, 0])
        h_p, h_s = seqs(h)
        shift_p.append(h_p[:, -1])
        shift_s.append(h_s[:, -1])

        a_rows = jnp.concatenate([h, state_shift[l]], axis=0).astype(BF16)
        proj = _mm(a_rows, w_in, tm=tm_proj, tn=512, b_lead=(l,), n=c_main)
        tail = _mm(a_rows, w_tail, tm=tm_proj, tn=n_tail + v_lora, b_lead=(l,))

        def lerp(cols, mu_cols):
            c_p, c_s = seqs(cols)
            prev = stream(_shift_rows(c_p, jnp.zeros_like(c_p[:, :1])),
                          _shift_rows(c_s, cols[n_tok:, None]))
            cur = cols[:n_tok]
            return cur + (prev - cur) * mu_cols

        u_p, u_s = seqs(proj[:, :pw])
        d_p, np_p = _pool_diffs(u_p, jnp.zeros((bp, POOL_BUF, pw), F32), 0)
        d_s, np_s = _pool_diffs(u_s, state_pool[l], PAST_LEN)
        pool_p.append(np_p)
        pool_s.append(np_s)
        pool_out = _pool_mm(stream(d_p, d_s).astype(BF16), w_pool, l, tm=tm_tok) * pool_scale[l]

        z = lerp(proj[:, pw:], mu[l, :3 * rw])
        zt = lerp(tail, mu_tail[l])
        r, k, v = z[:, :rw], z[:, rw:2 * rw], z[:, 2 * rw:]
        wd = zt[:, :w_lora]
        ad = zt[:, w_lora:w_lora + a_lora]
        gd = zt[:, w_lora + a_lora:n_tail]
        vd = zt[:, n_tail:]
        lora = functools.partial(_mm, tm=tm_tok, tn=512)
        w_log = -jax.nn.softplus(-(w_decay0[l] + lora(jnp.tanh(wd).astype(BF16), w_decay2, b_lead=(l,)))) - 0.5
        decay = jnp.exp(-jnp.exp(w_log))
        a = jax.nn.sigmoid(w_a0[l] + lora(ad.astype(BF16), w_a2, b_lead=(l,)))
        g = lora(jax.nn.sigmoid(gd).astype(BF16), w_g2, b_lead=(l,))
        if l == 0:
            v_first = v
        else:
            v = v + (v_first - v) * jax.nn.sigmoid(
                w_v0[l - 1] + lora(vd.astype(BF16), w_v2, b_lead=(l - 1,)))
        heads = lambda t_: t_.reshape(n_tok, n_heads, HEAD)
        kk = heads(k * k_k[l])
        kk = kk / jnp.maximum(jnp.sqrt(jnp.sum(kk * kk, axis=-1, keepdims=True)), 1e-12)
        kk = kk.reshape(n_tok, rw)
        k_mod = k * (1 + (a - 1) * k_a[l])
        kka = kk * a

        ins = [seqs(t_) for t_ in (r, decay, k_mod, v, kk, kka)]
        o_p, s_p = _wkv(*[i_[0] for i_ in ins], zero_state, tc=min(tp, 128))
        o_s, s_s = _wkv(*[i_[1] for i_ in ins], state_wkv[l], tc=ts)
        wkv_p.append(s_p)
        wkv_s.append(s_s)
        o = heads(stream(o_p, o_s))
        mean = jnp.mean(o, axis=-1, keepdims=True)
        var = jnp.mean(jnp.square(o - mean), axis=-1, keepdims=True)
        o = ((o - mean) * lax.rsqrt(var + GN_EPS)).reshape(n_tok, rw) * ln_x_w[l] + ln_x_b[l]
        bonus = jnp.sum(heads(r) * heads(k_mod) * r_k[l], axis=-1, keepdims=True) * heads(v)
        o = (o + bonus.reshape(n_tok, rw)) * g
        mixed = _mm(jnp.concatenate([pool_out, o], axis=-1).astype(BF16), w_out,
                    tm=tm_tok, tn=512, b_lead=(l,))
        x = x + per_token(m[:, 2]) * mixed

        h2 = _rms_norm(x, g_ffn[l]) * (1 + per_token(m[:, 4])) + per_token(m[:, 3])
        moe = _hier_moe(h2, l, w_route, b_route_group, b_route_expert,
                        w_exp_gate, w_exp_up, w_exp_down, tm_route=tm_tok // 2)
        x = x + per_token(m[:, 5]) * moe

    y = _rms_norm(x, g_final)
    y_p, y_s = seqs(y)
    return (y_p, y_s, jnp.stack(shift_p), jnp.stack(pool_p), jnp.stack(wkv_p),
            jnp.stack(shift_s), jnp.stack(pool_s), jnp.stack(wkv_s))
```

```python
import functools

import jax
import jax.numpy as jnp
from jax import lax
from jax.experimental import pallas as pl
from jax.experimental.pallas import tpu as pltpu

F32 = jnp.float32
BF16 = jnp.bfloat16

HEAD = 64
PAIR = 2 * HEAD
SUBLANES = 8
POOL_WINDOWS = (2, 4, 8, 16)
POOL_BUF = max(POOL_WINDOWS) - 1
EXP_PER_GROUP = 8
TOP_K = 2
RMS_EPS = 1e-6
GN_EPS = 64e-5
MOE_TM = 256
PAST_LEN = 16384
WKV_PAIRS = 8


def _dot_bf16(a_ref, b_ref):
    return jnp.dot(a_ref[...].astype(BF16), b_ref[...].astype(BF16), preferred_element_type=F32)


def _mm_kernel(a_ref, b_ref, o_ref):
    o_ref[...] = _dot_bf16(a_ref, b_ref).astype(o_ref.dtype)


def _mm(a, b, *, tm, tn, b_lead=(), n=None, m=None, out_dtype=F32):
    k = a.shape[1]
    m = a.shape[0] if m is None else m
    n = b.shape[-1] if n is None else n
    assert b.shape[-2] == k and m % tm == 0 and n % tn == 0, (a.shape, b.shape, m, tm, n, tn)
    lead = tuple(b_lead)
    return pl.pallas_call(
        _mm_kernel,
        grid=(m // tm, n // tn),
        in_specs=[pl.BlockSpec((tm, k), lambda i, j: (i, 0)),
                  pl.BlockSpec((None,) * len(lead) + (k, tn), lambda i, j: lead + (0, j))],
        out_specs=pl.BlockSpec((tm, tn), lambda i, j: (i, j)),
        out_shape=jax.ShapeDtypeStruct((m, n), out_dtype),
        compiler_params=pltpu.CompilerParams(dimension_semantics=("parallel", "arbitrary")),
        name="mm",
    )(a, b)


def _mm_lerp_kernel(a_ref, b_ref, mu_ref, o_ref, carry_ref, *, tiles_per_seq):
    i, j = pl.program_id(0), pl.program_id(1)
    acc = _dot_bf16(a_ref, b_ref)
    prev = pltpu.roll(acc, 1, axis=0)
    if tiles_per_seq is not None:
        @pl.when(i == 0)
        def _():
            carry_ref[j] = jnp.zeros(carry_ref.shape[1:], F32)

        row = lax.broadcasted_iota(jnp.int32, acc.shape, 0)
        last = jnp.where(i % tiles_per_seq == 0, 0., carry_ref[j][SUBLANES - 1:])
        prev = jnp.where(row == 0, last, prev)
        carry_ref[j] = acc[acc.shape[0] - SUBLANES:]
    o_ref[...] = acc + (prev - acc) * mu_ref[...]


def _mm_lerp(a, b, mu_cols, layer, *, tm, tn, m, n, tiles_per_seq):
    k = a.shape[1]
    assert m % tm == 0 and n % tn == 0
    return pl.pallas_call(
        functools.partial(_mm_lerp_kernel, tiles_per_seq=tiles_per_seq),
        grid=(m // tm, n // tn),
        in_specs=[pl.BlockSpec((tm, k), lambda i, j: (i, 0)),
                  pl.BlockSpec((None, k, tn), lambda i, j: (layer, 0, j)),
                  pl.BlockSpec((None, 1, tn), lambda i, j: (layer, 0, j))],
        out_specs=pl.BlockSpec((tm, tn), lambda i, j: (i, j)),
        out_shape=jax.ShapeDtypeStruct((m, n), F32),
        scratch_shapes=[pltpu.VMEM((n // tn, SUBLANES, tn), F32)],
        compiler_params=pltpu.CompilerParams(dimension_semantics=("arbitrary", "arbitrary")),
        name="proj_lerp",
    )(a, b, mu_cols)


def _pool_mm_kernel(a_ref, b_ref, s_ref, o_ref):
    o_ref[...] = (_dot_bf16(a_ref, b_ref) * s_ref[...]).astype(o_ref.dtype)


def _pool_mm(d, w_pool, scale, layer, *, tm):
    m = d.shape[0]
    g, c = w_pool.shape[1], w_pool.shape[2]
    return pl.pallas_call(
        _pool_mm_kernel,
        grid=(m // tm, g),
        in_specs=[pl.BlockSpec((tm, c), lambda i, j: (i, j)),
                  pl.BlockSpec((None, None, c, c), lambda i, j: (layer, j, 0, 0)),
                  pl.BlockSpec((None, 1, c), lambda i, j: (layer, 0, j))],
        out_specs=pl.BlockSpec((tm, c), lambda i, j: (i, j)),
        out_shape=jax.ShapeDtypeStruct((m, g * c), BF16),
        compiler_params=pltpu.CompilerParams(dimension_semantics=("parallel", "arbitrary")),
        name="pool_mm",
    )(d, w_pool, scale)


def _mix_out_kernel(p_ref, o1_ref, o2_ref, o3_ref, w0_ref, w1_ref, w2_ref, w3_ref, x_ref, g_ref, y_ref):
    acc = (_dot_bf16(p_ref, w0_ref) + _dot_bf16(o1_ref, w1_ref)
           + _dot_bf16(o2_ref, w2_ref) + _dot_bf16(o3_ref, w3_ref))
    y_ref[...] = x_ref[...] + g_ref[...] * acc


def _mix_out(pool_out, o, w_out, x, gate, layer, *, tm, tn):
    n, d = x.shape
    q = pool_out.shape[1]
    assert o.shape[1] == 3 * q and d == 4 * q
    w4 = w_out.reshape(w_out.shape[0], 4, q, d)
    a_spec = lambda c: pl.BlockSpec((tm, q), lambda i, j: (i, c))
    w_spec = lambda c: pl.BlockSpec((None, None, q, tn), lambda i, j: (layer, c, 0, j))
    tile = pl.BlockSpec((tm, tn), lambda i, j: (i, j))
    return pl.pallas_call(
        _mix_out_kernel,
        grid=(n // tm, d // tn),
        in_specs=[a_spec(0), a_spec(0), a_spec(1), a_spec(2),
                  w_spec(0), w_spec(1), w_spec(2), w_spec(3), tile, tile],
        out_specs=tile,
        out_shape=jax.ShapeDtypeStruct((n, d), F32),
        compiler_params=pltpu.CompilerParams(dimension_semantics=("parallel", "arbitrary")),
        name="mix_out",
    )(pool_out, o, o, o, w4, w4, w4, w4, x, gate)


def _seg_sum(x, ones_blockdiag):
    hi = x.astype(BF16)
    lo = (x - hi.astype(F32)).astype(BF16)
    return (jnp.dot(hi, ones_blockdiag, preferred_element_type=F32)
            + jnp.dot(lo, ones_blockdiag, preferred_element_type=F32))


def _wkv_kernel(r_ref, d_ref, k_ref, v_ref, kk_ref, kka_ref, s0_ref, o_ref, s_ref, *, tc, npairs):
    @pl.when(pl.program_id(1) == 0)
    def _():
        s_ref[...] = s0_ref[...]

    lane = lax.broadcasted_iota(jnp.int32, (HEAD, PAIR), 1)
    row = lax.broadcasted_iota(jnp.int32, (HEAD, PAIR), 0)
    diag = ((lane & (HEAD - 1)) == row).astype(F32)
    jr = lax.broadcasted_iota(jnp.int32, (PAIR, PAIR), 0) >= HEAD
    jc = lax.broadcasted_iota(jnp.int32, (PAIR, PAIR), 1) >= HEAD
    ones_blockdiag = (jr == jc).astype(BF16)

    rows = min(tc, SUBLANES)
    row_id = lax.broadcasted_iota(jnp.int32, (rows, PAIR), 0)

    def row_group(gi, carry):
        base = pl.multiple_of(gi * rows, rows)

        def rowvec(ref, p, i):
            return ref[0, pl.ds(base, rows), p * PAIR:(p + 1) * PAIR][i:i + 1]

        o_tiles = [jnp.zeros((rows, PAIR), F32)] * npairs
        for i in range(rows):
            xs = []
            for p in range(npairs):
                xs.append(s_ref[0, p] * rowvec(kk_ref, p, i))
                xs.append(diag * rowvec(v_ref, p, i))
            y = _seg_sum(jnp.concatenate(xs, axis=0), ones_blockdiag)
            qs = []
            for p in range(npairs):
                s_kk = y[p * PAIR:p * PAIR + HEAD]
                v_col = y[p * PAIR + HEAD:(p + 1) * PAIR]
                s = (s_ref[0, p] * rowvec(d_ref, p, i) - s_kk * rowvec(kka_ref, p, i)
                     + v_col * rowvec(k_ref, p, i))
                s_ref[0, p] = s
                qs.append(s * rowvec(r_ref, p, i))
            z = _seg_sum(jnp.concatenate(qs, axis=0), ones_blockdiag)
            for p in range(npairs):
                o_col = z[p * HEAD:(p + 1) * HEAD]
                o_row = jnp.sum(o_col * diag, axis=0, keepdims=True)
                o_tiles[p] = jnp.where(row_id == i, o_row, o_tiles[p])
        for p in range(npairs):
            o_ref[0, pl.ds(base, rows), p * PAIR:(p + 1) * PAIR] = o_tiles[p]
        return carry

    lax.fori_loop(0, tc // rows, row_group, 0)


def _pair_state(s):
    b, h = s.shape[:2]
    return s.reshape(b, h // 2, 2, HEAD, HEAD).transpose(0, 1, 3, 2, 4).reshape(b, h // 2, HEAD, PAIR)


def _unpair_state(s):
    b, hp = s.shape[:2]
    return s.reshape(b, hp, HEAD, 2, HEAD).transpose(0, 1, 3, 2, 4).reshape(b, 2 * hp, HEAD, HEAD)


def _wkv(r, decay, k, v, kk, kka, s0, *, tc):
    b, t, w = r.shape
    npairs = w // PAIR
    assert t % tc == 0
    seq = pl.BlockSpec((1, tc, w), lambda i, c: (i, c, 0))
    st = pl.BlockSpec((1, npairs, HEAD, PAIR), lambda i, c: (i, 0, 0, 0))
    o, s = pl.pallas_call(
        functools.partial(_wkv_kernel, tc=tc, npairs=npairs),
        grid=(b, t // tc),
        in_specs=[seq] * 6 + [st],
        out_specs=[seq, st],
        out_shape=[jax.ShapeDtypeStruct((b, t, w), F32),
                   jax.ShapeDtypeStruct((b, npairs, HEAD, PAIR), F32)],
        compiler_params=pltpu.CompilerParams(dimension_semantics=("parallel", "arbitrary")),
        name="wkv_steps",
    )(r, decay, k, v, kk, kka, _pair_state(s0))
    return o, _unpair_state(s)


def _bdot(x, y):
    return jnp.dot(x.astype(BF16), y.astype(BF16), preferred_element_type=F32)


def _bdot_nt(x, y):
    return lax.dot_general(x.astype(BF16), y.astype(BF16), (((1,), (1,)), ((), ())),
                           preferred_element_type=F32)


def _split3(x):
    h1 = x.astype(BF16)
    r1 = x - h1.astype(F32)
    h2 = r1.astype(BF16)
    return h1, h2, (r1 - h2.astype(F32)).astype(BF16)


def _wkv_chunk_kernel(r_ref, ld_ref, k_ref, v_ref, kk_ref, kka_ref, g_ref, rk_ref, lnw_ref, lnb_ref,
                      s0_ref, o_ref, s_ref, *, npairs):
    c = HEAD

    @pl.when(pl.program_id(2) == 0)
    def _():
        s_ref[...] = s0_ref[...]

    row = lax.broadcasted_iota(jnp.int32, (c, PAIR), 0)
    col = lax.broadcasted_iota(jnp.int32, (c, PAIR), 1)
    left = col < c
    strict = (col & (c - 1)) < row
    incl = (col & (c - 1)) <= row
    tri = (lax.broadcasted_iota(jnp.int32, (c, c), 1)
           <= lax.broadcasted_iota(jnp.int32, (c, c), 0)).astype(BF16)
    same_head = ((lax.broadcasted_iota(jnp.int32, (PAIR, PAIR), 0) >= HEAD)
                 == (lax.broadcasted_iota(jnp.int32, (PAIR, PAIR), 1) >= HEAD))
    ones_blockdiag = same_head.astype(BF16)

    def halves(x):
        return jnp.concatenate([jnp.where(left, x, 0.), jnp.where(left, 0., x)], axis=1)

    def pairs(x):
        return jnp.stack([x[:, p * PAIR:(p + 1) * PAIR] for p in range(npairs)])

    def bmm(x, y):
        return lax.dot_general(x.astype(BF16), y.astype(BF16), (((2,), (1,)), ((0,), (0,))),
                               preferred_element_type=F32)

    def bmm_nt(x, y):
        return lax.dot_general(x.astype(BF16), y.astype(BF16), (((2,), (2,)), ((0,), (0,))),
                               preferred_element_type=F32)

    def seg_sum(x):
        return _seg_sum(x.reshape(npairs * c, PAIR), ones_blockdiag).reshape(npairs, c, PAIR)

    ld = ld_ref[0]
    r, k, v = r_ref[0], k_ref[0], v_ref[0]
    b = sum(jnp.dot(tri, part, preferred_element_type=F32) for part in _split3(ld))
    eb = jnp.exp(b)
    e_inv = jnp.exp(-b)
    a_t = pairs(-kk_ref[0] * jnp.exp(b - ld))
    r_t = pairs(r * eb)
    k_t = pairs(k * e_inv)
    b_t = pairs(kka_ref[0] * e_inv)
    eb_last = pairs(eb[c - 1:c])
    v_p = pairs(v)
    s = s_ref[0]

    gram = bmm_nt(jnp.concatenate([a_t, r_t], axis=1),
                  jnp.concatenate([halves(b_t), halves(k_t)], axis=1))
    l_ab = jnp.where(strict, gram[:, :c, :PAIR], 0.)
    l_ak = jnp.where(strict, gram[:, :c, PAIR:], 0.)
    m_rb = jnp.where(incl, gram[:, c:, :PAIR], 0.)
    m_rk = jnp.where(incl, gram[:, c:, PAIR:], 0.)

    v_h = halves(v_p)
    u = bmm_nt(a_t, s) + bmm(l_ak, v_h)
    l_pow = l_ab
    n = 1
    while True:
        u = u + bmm(l_pow, halves(u))
        n *= 2
        if n >= c:
            break
        l_pow = bmm(l_pow, halves(l_pow))
    o = bmm_nt(r_t, s) + bmm(jnp.concatenate([m_rk, m_rb], axis=2),
                             jnp.concatenate([v_h, halves(u)], axis=1))
    vu = jnp.concatenate([v_p, u], axis=1)
    vu_t = jnp.stack([vu[p].T for p in range(npairs)])
    s_new = bmm(vu_t, jnp.concatenate([k_t, b_t], axis=1))
    s_ref[0] = (s + jnp.where(same_head, s_new, 0.)) * eb_last

    mean = seg_sum(o) * (1.0 / HEAD)
    dev = o - mean
    var = seg_sum(dev * dev) * (1.0 / HEAD)
    bonus = seg_sum(pairs(r * k * rk_ref[...])) * v_p
    o = dev * lax.rsqrt(var + GN_EPS) * pairs(lnw_ref[...]) + pairs(lnb_ref[...]) + bonus
    o = o * pairs(g_ref[0])
    for p in range(npairs):
        o_ref[0, :, p * PAIR:(p + 1) * PAIR] = o[p].astype(o_ref.dtype)


def _blockdiag_state(s):
    b, h = s.shape[:2]
    s = s.reshape(b, h // 2, 2, HEAD, HEAD)
    z = jnp.zeros_like(s[:, :, 0])
    return jnp.concatenate([jnp.concatenate([s[:, :, 0], z], axis=-1),
                            jnp.concatenate([z, s[:, :, 1]], axis=-1)], axis=-2)


def _unblockdiag_state(s):
    b, hp = s.shape[:2]
    return jnp.stack([s[:, :, :HEAD, :HEAD], s[:, :, HEAD:, HEAD:]], axis=2).reshape(
        b, 2 * hp, HEAD, HEAD)


def _wkv_chunked(r, log_decay, k, v, kk, kka, g, r_k, ln_w, ln_b, s0, *, npairs):
    b, t, w = r.shape
    assert t % HEAD == 0 and (w // PAIR) % npairs == 0
    wb = npairs * PAIR
    seq = pl.BlockSpec((1, HEAD, wb), lambda i, j, c: (i, c, j))
    vec = pl.BlockSpec((1, wb), lambda i, j, c: (0, j))
    st = pl.BlockSpec((1, npairs, PAIR, PAIR), lambda i, j, c: (i, j, 0, 0))
    o, s = pl.pallas_call(
        functools.partial(_wkv_chunk_kernel, npairs=npairs),
        grid=(b, w // wb, t // HEAD),
        in_specs=[seq] * 7 + [vec] * 3 + [st],
        out_specs=[seq, st],
        out_shape=[jax.ShapeDtypeStruct((b, t, w), BF16),
                   jax.ShapeDtypeStruct((b, w // PAIR, PAIR, PAIR), F32)],
        compiler_params=pltpu.CompilerParams(
            dimension_semantics=("parallel", "parallel", "arbitrary")),
        name="wkv_chunked",
    )(r, log_decay, k, v, kk, kka, g, r_k.reshape(1, w), ln_w.reshape(1, w), ln_b.reshape(1, w),
      _blockdiag_state(s0))
    return o, _unblockdiag_state(s)


def _moe_up_kernel(te_ref, tv_ref, x_ref, wg_ref, wu_ref, h_ref):
    i = pl.program_id(0)

    @pl.when(tv_ref[i] > 0)
    def _():
        x = x_ref[...]
        g = jnp.dot(x, wg_ref[...].astype(BF16), preferred_element_type=F32)
        u = jnp.dot(x, wu_ref[...].astype(BF16), preferred_element_type=F32)
        h_ref[...] = (g * jax.nn.sigmoid(g) * u).astype(h_ref.dtype)

    @pl.when(tv_ref[i] == 0)
    def _():
        h_ref[...] = jnp.zeros_like(h_ref)


def _moe_down_kernel(te_ref, tv_ref, h_ref, wd_ref, w_ref, y_ref):
    i = pl.program_id(0)

    @pl.when(tv_ref[i] > 0)
    def _():
        y = jnp.dot(h_ref[...], wd_ref[...].astype(BF16), preferred_element_type=F32)
        y_ref[...] = y * w_ref[...]

    @pl.when(tv_ref[i] == 0)
    def _():
        y_ref[...] = jnp.zeros_like(y_ref)


def _moe_experts(xs, row_w, tile_expert, tile_valid, w_gate, w_up, w_down, layer):
    p, d = xs.shape
    de = w_gate.shape[-1]
    nt = p // MOE_TM
    hid = pl.pallas_call(
        _moe_up_kernel,
        grid_spec=pltpu.PrefetchScalarGridSpec(
            num_scalar_prefetch=2, grid=(nt,),
            in_specs=[pl.BlockSpec((MOE_TM, d), lambda i, te, tv: (i, 0)),
                      pl.BlockSpec((None, None, d, de), lambda i, te, tv: (layer, te[i], 0, 0)),
                      pl.BlockSpec((None, None, d, de), lambda i, te, tv: (layer, te[i], 0, 0))],
            out_specs=pl.BlockSpec((MOE_TM, de), lambda i, te, tv: (i, 0))),
        out_shape=jax.ShapeDtypeStruct((p, de), BF16),
        compiler_params=pltpu.CompilerParams(dimension_semantics=("arbitrary",)),
        name="moe_up",
    )(tile_expert, tile_valid, xs, w_gate, w_up)
    return pl.pallas_call(
        _moe_down_kernel,
        grid_spec=pltpu.PrefetchScalarGridSpec(
            num_scalar_prefetch=2, grid=(nt,),
            in_specs=[pl.BlockSpec((MOE_TM, de), lambda i, te, tv: (i, 0)),
                      pl.BlockSpec((None, None, de, d), lambda i, te, tv: (layer, te[i], 0, 0)),
                      pl.BlockSpec((MOE_TM, 1), lambda i, te, tv: (i, 0))],
            out_specs=pl.BlockSpec((MOE_TM, d), lambda i, te, tv: (i, 0))),
        out_shape=jax.ShapeDtypeStruct((p, d), F32),
        compiler_params=pltpu.CompilerParams(dimension_semantics=("arbitrary",)),
        name="moe_down",
    )(tile_expert, tile_valid, hid, w_down, row_w)


def _hier_moe(h2, layer, w_route, b_route_group, b_route_expert, w_gate, w_up, w_down, *, tm_route):
    n, d = h2.shape
    n_groups = b_route_group.shape[-1]
    n_experts = b_route_expert.shape[-1]
    logits = _mm(h2, w_route, tm=tm_route, tn=w_route.shape[-1], b_lead=(layer,))
    p_group = jax.nn.softmax(logits[:, :n_groups] + b_route_group[layer], axis=-1)
    p_top, g_idx = lax.top_k(p_group, 1)
    le = (logits[:, n_groups:n_groups + n_experts] + b_route_expert[layer])
    le = le.reshape(n, n_groups, EXP_PER_GROUP)
    le = jnp.take_along_axis(le, g_idx[:, :, None], axis=1)[:, 0]
    v_top, e_idx = lax.top_k(le, TOP_K)
    wts = p_top * jax.nn.softmax(v_top, axis=-1)
    gidx = (g_idx * EXP_PER_GROUP + e_idx).astype(jnp.int32)

    na = n * TOP_K
    nt = na // MOE_TM + n_experts
    e_flat = gidx.reshape(na)
    order = jnp.argsort(e_flat, stable=True).astype(jnp.int32)
    sorted_e = e_flat[order]
    counts = jnp.zeros((n_experts,), jnp.int32).at[e_flat].add(1)
    tiles_per = (counts + MOE_TM - 1) // MOE_TM
    tile_end = jnp.cumsum(tiles_per)
    pad_start = (tile_end - tiles_per) * MOE_TM
    count_start = jnp.cumsum(counts) - counts
    dest = pad_start[sorted_e] + jnp.arange(na, dtype=jnp.int32) - count_start[sorted_e]
    row_token = jnp.zeros((nt * MOE_TM,), jnp.int32).at[dest].set(order // TOP_K)
    row_w = jnp.zeros((nt * MOE_TM,), F32).at[dest].set(wts.reshape(na)[order])
    pos = jnp.zeros((na,), jnp.int32).at[order].set(dest)
    tile_ids = jnp.arange(nt, dtype=jnp.int32)
    tile_expert = jnp.minimum(jnp.searchsorted(tile_end, tile_ids, side='right'),
                              n_experts - 1).astype(jnp.int32)
    tile_valid = (tile_ids < tile_end[-1]).astype(jnp.int32)

    xs = jnp.take(h2, row_token, axis=0)
    ys = _moe_experts(xs, row_w[:, None], tile_expert, tile_valid, w_gate, w_up, w_down, layer)
    picked = jnp.take(ys, pos, axis=0).reshape(n, TOP_K, d)
    return picked[:, 0] + picked[:, 1]


def _rms_norm(x, g):
    return x * lax.rsqrt(jnp.mean(x * x, axis=-1, keepdims=True) + RMS_EPS) * g


def _pool_diffs(u, u_past, start_pos):
    b, t, pw = u.shape
    grp = pw // len(POOL_WINDOWS)
    ext = jnp.concatenate([u_past, u], axis=1)
    pos = start_pos + jnp.arange(t)
    diffs = []
    for gi, win in enumerate(POOL_WINDOWS):
        e = ext[:, :, gi * grp:(gi + 1) * grp]
        win_sum = e[:, POOL_BUF:POOL_BUF + t]
        for i in range(1, win):
            win_sum = win_sum + e[:, POOL_BUF - i:POOL_BUF - i + t]
        cnt = jnp.minimum(win, pos + 1).astype(F32)[None, :, None]
        diffs.append(win_sum / cnt - e[:, POOL_BUF:])
    return jnp.concatenate(diffs, axis=-1).reshape(b * t, pw), ext[:, t:]


def kernel(x_prompt, x_sample, state_shift, state_pool, state_wkv, c_prompt, c_sample, w_ada, b_ada, g_mix, g_ffn, g_final, w_in, w_vres_down, mu, mu_vres, w_pool, pool_scale, w_decay0, w_decay2, w_a0, w_a2, w_g2, w_v0, w_v2, k_k, k_a, r_k, ln_x_w, ln_x_b, w_out, w_route_group, b_route_group, w_route_expert, b_route_expert, w_exp_gate, w_exp_up, w_exp_down):
    bp, tp, d = x_prompt.shape
    bs, ts, _ = x_sample.shape
    depth = w_in.shape[0]
    n_heads = state_wkv.shape[2]
    rw = n_heads * HEAD
    pw = state_pool.shape[-1]
    n_mod = w_ada.shape[1] // d
    n_p, n_s = bp * tp, bs * ts
    n_tok = n_p + n_s
    c_main = pw + 3 * rw
    n_tail = w_in.shape[-1] - c_main
    v_lora = w_vres_down.shape[-1]
    w_lora, a_lora = w_decay2.shape[1], w_a2.shape[1]

    tm_tok = n_tok // 8
    tm_p = tp // 2
    npairs = max(q for q in range(1, WKV_PAIRS + 1) if (n_heads // 2) % q == 0)

    def stream(a_p, a_s):
        return jnp.concatenate([a_p.reshape(n_p, -1), a_s.reshape(n_s, -1)], axis=0)

    c_all = jnp.concatenate([c_prompt, c_sample], axis=0)
    rows_c = -(-c_all.shape[0] // 16) * 16
    c_act = jnp.pad(jax.nn.silu(c_all), ((0, rows_c - c_all.shape[0]), (0, 0))).astype(BF16)
    mod = _mm(c_act, w_ada, tm=rows_c, tn=512)[:bp + bs].reshape(bp + bs, n_mod, d)

    def per_token(vec):
        return stream(jnp.broadcast_to(vec[:bp, None], (bp, tp, d)),
                      jnp.broadcast_to(vec[bp:, None], (bs, ts, d)))

    zeros_l = lambda *shape: jnp.zeros((1,) + shape, F32)
    w_tail = jnp.concatenate(
        [w_in[:, :, c_main:], jnp.concatenate([zeros_l(d, v_lora), w_vres_down], axis=0)], axis=-1)
    mu_main = jnp.concatenate([jnp.zeros((depth, pw), F32), mu[:, :3 * rw]], axis=-1)[:, None]
    mu_tail = jnp.concatenate(
        [mu[:, 3 * rw:], jnp.concatenate([zeros_l(v_lora), mu_vres], axis=0)], axis=-1)[:, None]
    route_pad = 128 - (w_route_group.shape[-1] + w_route_expert.shape[-1])
    w_route = jnp.concatenate(
        [w_route_group, w_route_expert, jnp.zeros((depth, d, route_pad), F32)], axis=-1)
    pool_scale3 = pool_scale[:, None]

    def mixer_inputs(z, zt, v_first, l, tm):
        n = z.shape[0]
        r, k, v = z[:, pw:pw + rw], z[:, pw + rw:pw + 2 * rw], z[:, pw + 2 * rw:]
        wd = zt[:, :w_lora]
        ad = zt[:, w_lora:w_lora + a_lora]
        gd = zt[:, w_lora + a_lora:n_tail]
        vd = zt[:, n_tail:]
        lora = functools.partial(_mm, tm=tm, tn=512)
        w_log = -jax.nn.softplus(
            -(w_decay0[l] + lora(jnp.tanh(wd).astype(BF16), w_decay2, b_lead=(l,)))) - 0.5
        log_decay = -jnp.exp(w_log)
        a = jax.nn.sigmoid(w_a0[l] + lora(ad.astype(BF16), w_a2, b_lead=(l,)))
        g = lora(jax.nn.sigmoid(gd).astype(BF16), w_g2, b_lead=(l,))
        if l > 0:
            v = v + (v_first - v) * jax.nn.sigmoid(
                w_v0[l - 1] + lora(vd.astype(BF16), w_v2, b_lead=(l - 1,)))
        kk = (k * k_k[l]).reshape(n, n_heads, HEAD)
        kk = kk / jnp.maximum(jnp.sqrt(jnp.sum(kk * kk, axis=-1, keepdims=True)), 1e-12)
        kk = kk.reshape(n, rw)
        k_mod = k * (1 + (a - 1) * k_a[l])
        return r, log_decay, k_mod, v, kk, kk * a, g

    x = stream(x_prompt, x_sample)
    vf_p = vf_s = None
    shift_p, pool_p, wkv_p, shift_s, pool_s, wkv_s = [], [], [], [], [], []
    zero_state = jnp.zeros((bp, n_heads, HEAD, HEAD), F32)
    for l in range(depth):
        m = mod + b_ada[l]
        h = _rms_norm(x, g_mix[l]) * (1 + per_token(m[:, 1])) + per_token(m[:, 0])
        h_p, h_s = h[:n_p].reshape(bp, tp, d), h[n_p:].reshape(bs, ts, d)
        shift_p.append(h_p[:, -1])
        shift_s.append(h_s[:, -1])
        h_bf = h.astype(BF16)
        hs_bf = jnp.concatenate([state_shift[l][:, None], h_s], axis=1).reshape(
            bs * (ts + 1), d).astype(BF16)

        z_p = _mm_lerp(h_bf, w_in, mu_main, l, tm=tm_p, tn=512, m=n_p, n=c_main,
                       tiles_per_seq=tp // tm_p)
        zt_p = _mm_lerp(h_bf, w_tail, mu_tail, l, tm=tm_p, tn=n_tail + v_lora, m=n_p,
                        n=n_tail + v_lora, tiles_per_seq=tp // tm_p)
        drop_first = lambda a_: a_.reshape(bs, ts + 1, -1)[:, 1:].reshape(n_s, -1)
        z_s = drop_first(_mm_lerp(hs_bf, w_in, mu_main, l, tm=hs_bf.shape[0], tn=512,
                                  m=hs_bf.shape[0], n=c_main, tiles_per_seq=None))
        zt_s = drop_first(_mm_lerp(hs_bf, w_tail, mu_tail, l, tm=hs_bf.shape[0],
                                   tn=n_tail + v_lora, m=hs_bf.shape[0], n=n_tail + v_lora,
                                   tiles_per_seq=None))

        d_p, np_p = _pool_diffs(z_p[:, :pw].reshape(bp, tp, pw), jnp.zeros((bp, POOL_BUF, pw), F32), 0)
        d_s, np_s = _pool_diffs(z_s[:, :pw].reshape(bs, ts, pw), state_pool[l], PAST_LEN)
        pool_p.append(np_p)
        pool_s.append(np_s)
        pool_out = _pool_mm(jnp.concatenate([d_p, d_s], axis=0).astype(BF16), w_pool, pool_scale3,
                            l, tm=tm_tok)

        in_p = mixer_inputs(z_p, zt_p, vf_p, l, tm_p)
        in_s = mixer_inputs(z_s, zt_s, vf_s, l, n_s)
        if l == 0:
            vf_p, vf_s = in_p[3], in_s[3]
        seq_p = lambda a_: a_.reshape(bp, tp, rw)
        seq_s = lambda a_: a_.reshape(bs, ts, rw)
        o_p, s_p = _wkv_chunked(*[seq_p(a_) for a_ in in_p], r_k[l], ln_x_w[l], ln_x_b[l],
                                zero_state, npairs=npairs)
        r_s, ld_s, k_s, v_s, kk_s, kka_s, g_s = in_s
        o_s, s_s = _wkv(seq_s(r_s), seq_s(jnp.exp(ld_s)), seq_s(k_s), seq_s(v_s), seq_s(kk_s),
                        seq_s(kka_s), state_wkv[l], tc=ts)
        wkv_p.append(s_p)
        wkv_s.append(s_s)
        heads = lambda a_: a_.reshape(n_s, n_heads, HEAD)
        o_s = heads(o_s)
        mean = jnp.mean(o_s, axis=-1, keepdims=True)
        var = jnp.mean(jnp.square(o_s - mean), axis=-1, keepdims=True)
        o_s = ((o_s - mean) * lax.rsqrt(var + GN_EPS)).reshape(n_s, rw) * ln_x_w[l] + ln_x_b[l]
        bonus = jnp.sum(heads(r_s) * heads(k_s) * r_k[l], axis=-1, keepdims=True) * heads(v_s)
        o_s = ((o_s + bonus.reshape(n_s, rw)) * g_s).astype(BF16)
        o = jnp.concatenate([o_p.reshape(n_p, rw), o_s], axis=0)
        x = _mix_out(pool_out, o, w_out, x, per_token(m[:, 2]), l, tm=tm_tok, tn=512)

        h2 = (_rms_norm(x, g_ffn[l]) * (1 + per_token(m[:, 4])) + per_token(m[:, 3])).astype(BF16)
        moe = _hier_moe(h2, l, w_route, b_route_group, b_route_expert,
                        w_exp_gate, w_exp_up, w_exp_down, tm_route=tm_tok)
        x = x + per_token(m[:, 5]) * moe

    y = _rms_norm(x, g_final)
    return (y[:n_p].reshape(bp, tp, d), y[n_p:].reshape(bs, ts, d),
            jnp.stack(shift_p), jnp.stack(pool_p), jnp.stack(wkv_p),
            jnp.stack(shift_s), jnp.stack(pool_s), jnp.stack(wkv_s))
```

```python
import functools
import math

import jax
import jax.numpy as jnp
from jax import lax
from jax.experimental import pallas as pl
from jax.experimental.pallas import tpu as pltpu

F32 = jnp.float32
BF16 = jnp.bfloat16

HEAD = 64
PAIR = 2 * HEAD
SUBLANES = 8
POOL_WINDOWS = (2, 4, 8, 16)
POOL_BUF = max(POOL_WINDOWS) - 1
EXP_PER_GROUP = 8
TOP_K = 2
RMS_EPS = 1e-6
GN_EPS = 64e-5
MOE_TM = 256
PAST_LEN = 16384
WKV_PAIRS = 8


def _dot_bf16(a_ref, b_ref):
    return jnp.dot(a_ref[...].astype(BF16), b_ref[...].astype(BF16), preferred_element_type=F32)


def _mm_kernel(a_ref, b_ref, o_ref):
    o_ref[...] = _dot_bf16(a_ref, b_ref).astype(o_ref.dtype)


def _mm(a, b, *, tm, tn, b_lead=(), n=None, m=None, out_dtype=F32):
    k = a.shape[1]
    m = a.shape[0] if m is None else m
    n = b.shape[-1] if n is None else n
    assert b.shape[-2] == k and m % tm == 0 and n % tn == 0, (a.shape, b.shape, m, tm, n, tn)
    lead = tuple(b_lead)
    return pl.pallas_call(
        _mm_kernel,
        grid=(m // tm, n // tn),
        in_specs=[pl.BlockSpec((tm, k), lambda i, j: (i, 0)),
                  pl.BlockSpec((None,) * len(lead) + (k, tn), lambda i, j: lead + (0, j))],
        out_specs=pl.BlockSpec((tm, tn), lambda i, j: (i, j)),
        out_shape=jax.ShapeDtypeStruct((m, n), out_dtype),
        compiler_params=pltpu.CompilerParams(dimension_semantics=("parallel", "arbitrary")),
        name="mm",
    )(a, b)


def _mm_lerp_kernel(a_ref, b_ref, mu_ref, o_ref, carry_ref, *, tiles_per_seq):
    i, j = pl.program_id(0), pl.program_id(1)
    acc = _dot_bf16(a_ref, b_ref)
    prev = pltpu.roll(acc, 1, axis=0)
    if tiles_per_seq is not None:
        @pl.when(i == 0)
        def _():
            carry_ref[j] = jnp.zeros(carry_ref.shape[1:], F32)

        row = lax.broadcasted_iota(jnp.int32, acc.shape, 0)
        last = jnp.where(i % tiles_per_seq == 0, 0., carry_ref[j][SUBLANES - 1:])
        prev = jnp.where(row == 0, last, prev)
        carry_ref[j] = acc[acc.shape[0] - SUBLANES:]
    o_ref[...] = acc + (prev - acc) * mu_ref[...]


def _mm_lerp(a, b, mu_cols, layer, *, tm, tn, m, n, tiles_per_seq):
    k = a.shape[1]
    assert m % tm == 0 and n % tn == 0
    return pl.pallas_call(
        functools.partial(_mm_lerp_kernel, tiles_per_seq=tiles_per_seq),
        grid=(m // tm, n // tn),
        in_specs=[pl.BlockSpec((tm, k), lambda i, j: (i, 0)),
                  pl.BlockSpec((None, k, tn), lambda i, j: (layer, 0, j)),
                  pl.BlockSpec((None, 1, tn), lambda i, j: (layer, 0, j))],
        out_specs=pl.BlockSpec((tm, tn), lambda i, j: (i, j)),
        out_shape=jax.ShapeDtypeStruct((m, n), F32),
        scratch_shapes=[pltpu.VMEM((n // tn, SUBLANES, tn), F32)],
        compiler_params=pltpu.CompilerParams(dimension_semantics=("arbitrary", "arbitrary")),
        name="proj_lerp",
    )(a, b, mu_cols)


def _pool_mm_kernel(a_ref, b_ref, s_ref, o_ref):
    o_ref[...] = (_dot_bf16(a_ref, b_ref) * s_ref[...]).astype(o_ref.dtype)


def _pool_mm(d, w_pool, scale, layer, *, tm):
    m = d.shape[0]
    g, c = w_pool.shape[1], w_pool.shape[2]
    return pl.pallas_call(
        _pool_mm_kernel,
        grid=(m // tm, g),
        in_specs=[pl.BlockSpec((tm, c), lambda i, j: (i, j)),
                  pl.BlockSpec((None, None, c, c), lambda i, j: (layer, j, 0, 0)),
                  pl.BlockSpec((None, 1, c), lambda i, j: (layer, 0, j))],
        out_specs=pl.BlockSpec((tm, c), lambda i, j: (i, j)),
        out_shape=jax.ShapeDtypeStruct((m, g * c), BF16),
        compiler_params=pltpu.CompilerParams(dimension_semantics=("parallel", "arbitrary")),
        name="pool_mm",
    )(d, w_pool, scale)


def _mix_out_kernel(p_ref, o1_ref, o2_ref, o3_ref, w0_ref, w1_ref, w2_ref, w3_ref, x_ref, g_ref, y_ref):
    acc = (_dot_bf16(p_ref, w0_ref) + _dot_bf16(o1_ref, w1_ref)
           + _dot_bf16(o2_ref, w2_ref) + _dot_bf16(o3_ref, w3_ref))
    y_ref[...] = x_ref[...] + g_ref[...] * acc


def _mix_out(pool_out, o, w_out, x, gate, layer, *, tm, tn):
    n, d = x.shape
    q = pool_out.shape[1]
    assert o.shape[1] == 3 * q and d == 4 * q
    w4 = w_out.reshape(w_out.shape[0], 4, q, d)
    a_spec = lambda c: pl.BlockSpec((tm, q), lambda i, j: (i, c))
    w_spec = lambda c: pl.BlockSpec((None, None, q, tn), lambda i, j: (layer, c, 0, j))
    tile = pl.BlockSpec((tm, tn), lambda i, j: (i, j))
    return pl.pallas_call(
        _mix_out_kernel,
        grid=(n // tm, d // tn),
        in_specs=[a_spec(0), a_spec(0), a_spec(1), a_spec(2),
                  w_spec(0), w_spec(1), w_spec(2), w_spec(3), tile, tile],
        out_specs=tile,
        out_shape=jax.ShapeDtypeStruct((n, d), F32),
        compiler_params=pltpu.CompilerParams(dimension_semantics=("parallel", "arbitrary")),
        name="mix_out",
    )(pool_out, o, o, o, w4, w4, w4, w4, x, gate)


def _norm_mod_kernel(x_ref, g_ref, sc_seq_ref, sh_seq_ref, sc_tok_ref, sh_tok_ref, o_ref, *, seq_tiles):
    x = x_ref[...]
    y = x * lax.rsqrt(jnp.mean(x * x, axis=-1, keepdims=True) + RMS_EPS) * g_ref[...]
    is_seq = pl.program_id(0) < seq_tiles
    scale = jnp.where(is_seq, sc_seq_ref[...], sc_tok_ref[...])
    shift = jnp.where(is_seq, sh_seq_ref[...], sh_tok_ref[...])
    o_ref[...] = (y * (1 + scale) + shift).astype(o_ref.dtype)


def _norm_mod(x, g, m_seq, m_tok, scale_i, shift_i, *, tm, rows_per_seq):
    n, d = x.shape
    n_seq = m_seq.shape[1]
    seq_tiles = n_seq * rows_per_seq // tm
    assert rows_per_seq % tm == 0 and m_tok.shape[1] % tm == 0 and n == n_seq * rows_per_seq + m_tok.shape[1]
    seq_of = lambda i: jnp.minimum(i * tm // rows_per_seq, n_seq - 1)
    tok_of = lambda i: jnp.maximum(i - seq_tiles, 0)
    seq_spec = lambda c: pl.BlockSpec((None, None, 1, d), lambda i: (c, seq_of(i), 0, 0))
    tok_spec = lambda c: pl.BlockSpec((None, tm, d), lambda i: (c, tok_of(i), 0))
    return pl.pallas_call(
        functools.partial(_norm_mod_kernel, seq_tiles=seq_tiles),
        grid=(n // tm,),
        in_specs=[pl.BlockSpec((tm, d), lambda i: (i, 0)), pl.BlockSpec((1, d), lambda i: (0, 0)),
                  seq_spec(scale_i), seq_spec(shift_i), tok_spec(scale_i), tok_spec(shift_i)],
        out_specs=pl.BlockSpec((tm, d), lambda i: (i, 0)),
        out_shape=jax.ShapeDtypeStruct((n, d), BF16),
        compiler_params=pltpu.CompilerParams(dimension_semantics=("parallel",)),
        name="norm_mod",
    )(x, g.reshape(1, d), m_seq, m_seq, m_tok, m_tok)


def _seg_sum(x, ones_blockdiag):
    hi = x.astype(BF16)
    lo = (x - hi.astype(F32)).astype(BF16)
    return (jnp.dot(hi, ones_blockdiag, preferred_element_type=F32)
            + jnp.dot(lo, ones_blockdiag, preferred_element_type=F32))


def _ones_blockdiag():
    same_head = ((lax.broadcasted_iota(jnp.int32, (PAIR, PAIR), 0) >= HEAD)
                 == (lax.broadcasted_iota(jnp.int32, (PAIR, PAIR), 1) >= HEAD))
    return same_head.astype(BF16)


def _prep_kernel(*refs, first_layer):
    if first_layer:
        (k_ref, v_ref, twd_ref, ad_ref, sgd_ref, wd2_ref, wa2_ref, wg2_ref, d0_ref, a0_ref, kk_w_ref,
         ka_ref, ld_o, kmod_o, v_o, kk_o, kka_o, g_o) = refs
    else:
        (k_ref, v_ref, twd_ref, ad_ref, sgd_ref, wd2_ref, wa2_ref, wg2_ref, d0_ref, a0_ref, kk_w_ref,
         ka_ref, vf_ref, vd_ref, wv2_ref, v0_ref, ld_o, kmod_o, v_o, kk_o, kka_o, g_o) = refs
    k = k_ref[...]
    y = -(d0_ref[...] + _dot_bf16(twd_ref, wd2_ref))
    softplus = jnp.maximum(y, 0.) + jnp.log(1. + jnp.exp(-jnp.abs(y)))
    ld_o[...] = -jnp.exp(-softplus - 0.5)
    a = jax.nn.sigmoid(a0_ref[...] + _dot_bf16(ad_ref, wa2_ref))
    g_o[...] = _dot_bf16(sgd_ref, wg2_ref)
    v = v_ref[...]
    if not first_layer:
        v = v + (vf_ref[...] - v) * jax.nn.sigmoid(v0_ref[...] + _dot_bf16(vd_ref, wv2_ref))
    v_o[...] = v
    kk = k * kk_w_ref[...]
    tm, wc = kk.shape
    nblk = wc // PAIR
    sq = jnp.concatenate([(kk * kk)[:, q * PAIR:(q + 1) * PAIR] for q in range(nblk)], axis=0)
    ss = _seg_sum(sq, _ones_blockdiag())
    ss = jnp.concatenate([ss[q * tm:(q + 1) * tm] for q in range(nblk)], axis=1)
    kk = kk / jnp.maximum(jnp.sqrt(ss), 1e-12)
    kk_o[...] = kk
    kka_o[...] = kk * a
    kmod_o[...] = k * (1 + (a - 1) * ka_ref[...])


def _prep(z, acts, v_first, weights, vecs, layer, *, tm, wc, k_off, v_off):
    n = z.shape[0]
    twd, ad, sgd, vd = acts
    w_d2, w_a2, w_g2, w_v2 = weights
    d0, a0, kk_w, ka, v0 = vecs
    rw = w_d2.shape[-1]
    first_layer = v_first is None
    col = lambda off: pl.BlockSpec((tm, wc), lambda i, j: (i, j + off))
    act = lambda a_: pl.BlockSpec((tm, a_.shape[1]), lambda i, j: (i, 0))
    lw = lambda w_, l_: pl.BlockSpec((None, w_.shape[1], wc), lambda i, j: (l_, 0, j))
    vec = lambda l_: pl.BlockSpec((None, 1, wc), lambda i, j: (l_, 0, j))
    args = [z, z, twd, ad, sgd, w_d2, w_a2, w_g2, d0, a0, kk_w, ka]
    specs = [col(k_off), col(v_off), act(twd), act(ad), act(sgd), lw(w_d2, layer), lw(w_a2, layer),
             lw(w_g2, layer), vec(layer), vec(layer), vec(layer), vec(layer)]
    if not first_layer:
        args += [v_first, vd, w_v2, v0]
        specs += [col(0), act(vd), lw(w_v2, layer - 1), vec(layer - 1)]
    out = jax.ShapeDtypeStruct((n, rw), F32)
    return pl.pallas_call(
        functools.partial(_prep_kernel, first_layer=first_layer),
        grid=(n // tm, rw // wc),
        in_specs=specs,
        out_specs=[col(0)] * 6,
        out_shape=[out] * 6,
        compiler_params=pltpu.CompilerParams(dimension_semantics=("parallel", "arbitrary")),
        name="rwkv_prep",
    )(*args)


def _wkv_kernel(r_ref, d_ref, k_ref, v_ref, kk_ref, kka_ref, s0_ref, o_ref, s_ref, *, tc, npairs):
    @pl.when(pl.program_id(1) == 0)
    def _():
        s_ref[...] = s0_ref[...]

    lane = lax.broadcasted_iota(jnp.int32, (HEAD, PAIR), 1)
    row = lax.broadcasted_iota(jnp.int32, (HEAD, PAIR), 0)
    diag = ((lane & (HEAD - 1)) == row).astype(F32)
    jr = lax.broadcasted_iota(jnp.int32, (PAIR, PAIR), 0) >= HEAD
    jc = lax.broadcasted_iota(jnp.int32, (PAIR, PAIR), 1) >= HEAD
    ones_blockdiag = (jr == jc).astype(BF16)

    rows = min(tc, SUBLANES)
    row_id = lax.broadcasted_iota(jnp.int32, (rows, PAIR), 0)

    def row_group(gi, carry):
        base = pl.multiple_of(gi * rows, rows)

        def rowvec(ref, p, i):
            return ref[0, pl.ds(base, rows), p * PAIR:(p + 1) * PAIR][i:i + 1]

        o_tiles = [jnp.zeros((rows, PAIR), F32)] * npairs
        for i in range(rows):
            xs = []
            for p in range(npairs):
                xs.append(s_ref[0, p] * rowvec(kk_ref, p, i))
                xs.append(diag * rowvec(v_ref, p, i))
            y = _seg_sum(jnp.concatenate(xs, axis=0), ones_blockdiag)
            qs = []
            for p in range(npairs):
                s_kk = y[p * PAIR:p * PAIR + HEAD]
                v_col = y[p * PAIR + HEAD:(p + 1) * PAIR]
                s = (s_ref[0, p] * rowvec(d_ref, p, i) - s_kk * rowvec(kka_ref, p, i)
                     + v_col * rowvec(k_ref, p, i))
                s_ref[0, p] = s
                qs.append(s * rowvec(r_ref, p, i))
            z = _seg_sum(jnp.concatenate(qs, axis=0), ones_blockdiag)
            for p in range(npairs):
                o_col = z[p * HEAD:(p + 1) * HEAD]
                o_row = jnp.sum(o_col * diag, axis=0, keepdims=True)
                o_tiles[p] = jnp.where(row_id == i, o_row, o_tiles[p])
        for p in range(npairs):
            o_ref[0, pl.ds(base, rows), p * PAIR:(p + 1) * PAIR] = o_tiles[p]
        return carry

    lax.fori_loop(0, tc // rows, row_group, 0)


def _pair_state(s):
    b, h = s.shape[:2]
    return s.reshape(b, h // 2, 2, HEAD, HEAD).transpose(0, 1, 3, 2, 4).reshape(b, h // 2, HEAD, PAIR)


def _unpair_state(s):
    b, hp = s.shape[:2]
    return s.reshape(b, hp, HEAD, 2, HEAD).transpose(0, 1, 3, 2, 4).reshape(b, 2 * hp, HEAD, HEAD)


def _wkv(r, decay, k, v, kk, kka, s0, *, tc):
    b, t, w = r.shape
    npairs = w // PAIR
    assert t % tc == 0
    seq = pl.BlockSpec((1, tc, w), lambda i, c: (i, c, 0))
    st = pl.BlockSpec((1, npairs, HEAD, PAIR), lambda i, c: (i, 0, 0, 0))
    o, s = pl.pallas_call(
        functools.partial(_wkv_kernel, tc=tc, npairs=npairs),
        grid=(b, t // tc),
        in_specs=[seq] * 6 + [st],
        out_specs=[seq, st],
        out_shape=[jax.ShapeDtypeStruct((b, t, w), F32),
                   jax.ShapeDtypeStruct((b, npairs, HEAD, PAIR), F32)],
        compiler_params=pltpu.CompilerParams(dimension_semantics=("parallel", "arbitrary")),
        name="wkv_steps",
    )(r, decay, k, v, kk, kka, _pair_state(s0))
    return o, _unpair_state(s)


def _bdot(x, y):
    return jnp.dot(x.astype(BF16), y.astype(BF16), preferred_element_type=F32)


def _bdot_nt(x, y):
    return lax.dot_general(x.astype(BF16), y.astype(BF16), (((1,), (1,)), ((), ())),
                           preferred_element_type=F32)


def _split3(x):
    h1 = x.astype(BF16)
    r1 = x - h1.astype(F32)
    h2 = r1.astype(BF16)
    return h1, h2, (r1 - h2.astype(F32)).astype(BF16)


def _wkv_chunk_kernel(r_ref, ld_ref, k_ref, v_ref, kk_ref, kka_ref, g_ref, rk_ref, lnw_ref, lnb_ref,
                      s0_ref, o_ref, s_ref, *, npairs):
    c = HEAD

    @pl.when(pl.program_id(2) == 0)
    def _():
        s_ref[...] = s0_ref[...]

    row = lax.broadcasted_iota(jnp.int32, (c, PAIR), 0)
    col = lax.broadcasted_iota(jnp.int32, (c, PAIR), 1)
    left = col < c
    strict = (col & (c - 1)) < row
    incl = (col & (c - 1)) <= row
    tri = (lax.broadcasted_iota(jnp.int32, (c, c), 1)
           <= lax.broadcasted_iota(jnp.int32, (c, c), 0)).astype(BF16)
    same_head = ((lax.broadcasted_iota(jnp.int32, (PAIR, PAIR), 0) >= HEAD)
                 == (lax.broadcasted_iota(jnp.int32, (PAIR, PAIR), 1) >= HEAD))
    ones_blockdiag = same_head.astype(BF16)

    def halves(x):
        return jnp.concatenate([jnp.where(left, x, 0.), jnp.where(left, 0., x)], axis=1)

    def pairs(x):
        return jnp.stack([x[:, p * PAIR:(p + 1) * PAIR] for p in range(npairs)])

    def bmm(x, y):
        return lax.dot_general(x.astype(BF16), y.astype(BF16), (((2,), (1,)), ((0,), (0,))),
                               preferred_element_type=F32)

    def bmm_nt(x, y):
        return lax.dot_general(x.astype(BF16), y.astype(BF16), (((2,), (2,)), ((0,), (0,))),
                               preferred_element_type=F32)

    def seg_sum(x):
        return _seg_sum(x.reshape(npairs * c, PAIR), ones_blockdiag).reshape(npairs, c, PAIR)

    ld = ld_ref[0]
    r, k, v = r_ref[0], k_ref[0], v_ref[0]
    b = sum(jnp.dot(tri, part, preferred_element_type=F32) for part in _split3(ld))
    eb = jnp.exp(b)
    e_inv = jnp.exp(-b)
    a_t = pairs(-kk_ref[0] * jnp.exp(b - ld))
    r_t = pairs(r * eb)
    k_t = pairs(k * e_inv)
    b_t = pairs(kka_ref[0] * e_inv)
    eb_last = pairs(eb[c - 1:c])
    v_p = pairs(v)
    s = s_ref[0]

    gram = bmm_nt(jnp.concatenate([a_t, r_t], axis=1),
                  jnp.concatenate([halves(b_t), halves(k_t)], axis=1))
    l_ab = jnp.where(strict, gram[:, :c, :PAIR], 0.)
    l_ak = jnp.where(strict, gram[:, :c, PAIR:], 0.)
    m_rb = jnp.where(incl, gram[:, c:, :PAIR], 0.)
    m_rk = jnp.where(incl, gram[:, c:, PAIR:], 0.)

    v_h = halves(v_p)
    u = bmm_nt(a_t, s) + bmm(l_ak, v_h)
    l_pow = l_ab
    n = 1
    while True:
        u = u + bmm(l_pow, halves(u))
        n *= 2
        if n >= c:
            break
        l_pow = bmm(l_pow, halves(l_pow))
    o = bmm_nt(r_t, s) + bmm(jnp.concatenate([m_rk, m_rb], axis=2),
                             jnp.concatenate([v_h, halves(u)], axis=1))
    vu = jnp.concatenate([v_p, u], axis=1)
    vu_t = jnp.stack([vu[p].T for p in range(npairs)])
    s_new = bmm(vu_t, jnp.concatenate([k_t, b_t], axis=1))
    s_ref[0] = (s + jnp.where(same_head, s_new, 0.)) * eb_last

    mean = seg_sum(o) * (1.0 / HEAD)
    dev = o - mean
    var = seg_sum(dev * dev) * (1.0 / HEAD)
    bonus = seg_sum(pairs(r * k * rk_ref[...])) * v_p
    o = dev * lax.rsqrt(var + GN_EPS) * pairs(lnw_ref[...]) + pairs(lnb_ref[...]) + bonus
    o = o * pairs(g_ref[0])
    for p in range(npairs):
        o_ref[0, :, p * PAIR:(p + 1) * PAIR] = o[p].astype(o_ref.dtype)


def _blockdiag_state(s):
    b, h = s.shape[:2]
    s = s.reshape(b, h // 2, 2, HEAD, HEAD)
    z = jnp.zeros_like(s[:, :, 0])
    return jnp.concatenate([jnp.concatenate([s[:, :, 0], z], axis=-1),
                            jnp.concatenate([z, s[:, :, 1]], axis=-1)], axis=-2)


def _unblockdiag_state(s):
    b, hp = s.shape[:2]
    return jnp.stack([s[:, :, :HEAD, :HEAD], s[:, :, HEAD:, HEAD:]], axis=2).reshape(
        b, 2 * hp, HEAD, HEAD)


def _wkv_chunked(r, log_decay, k, v, kk, kka, g, r_k, ln_w, ln_b, s0, *, npairs, r_off=0):
    b, t, w = log_decay.shape
    wb = npairs * PAIR
    assert t % HEAD == 0 and (w // PAIR) % npairs == 0 and r_off % wb == 0
    seq = pl.BlockSpec((1, HEAD, wb), lambda i, j, c: (i, c, j))
    r_seq = pl.BlockSpec((1, HEAD, wb), lambda i, j, c: (i, c, j + r_off // wb))
    vec = pl.BlockSpec((1, wb), lambda i, j, c: (0, j))
    st = pl.BlockSpec((1, npairs, PAIR, PAIR), lambda i, j, c: (i, j, 0, 0))
    o, s = pl.pallas_call(
        functools.partial(_wkv_chunk_kernel, npairs=npairs),
        grid=(b, w // wb, t // HEAD),
        in_specs=[r_seq] + [seq] * 6 + [vec] * 3 + [st],
        out_specs=[seq, st],
        out_shape=[jax.ShapeDtypeStruct((b, t, w), BF16),
                   jax.ShapeDtypeStruct((b, w // PAIR, PAIR, PAIR), F32)],
        compiler_params=pltpu.CompilerParams(
            dimension_semantics=("parallel", "parallel", "arbitrary")),
        name="wkv_chunked",
    )(r, log_decay, k, v, kk, kka, g, r_k.reshape(1, w), ln_w.reshape(1, w), ln_b.reshape(1, w),
      _blockdiag_state(s0))
    return o, _unblockdiag_state(s)


def _moe_up_kernel(te_ref, tv_ref, x_ref, wg_ref, wu_ref, h_ref):
    i = pl.program_id(0)

    @pl.when(tv_ref[i] > 0)
    def _():
        x = x_ref[...]
        g = jnp.dot(x, wg_ref[...].astype(BF16), preferred_element_type=F32)
        u = jnp.dot(x, wu_ref[...].astype(BF16), preferred_element_type=F32)
        h_ref[...] = (g * jax.nn.sigmoid(g) * u).astype(h_ref.dtype)

    @pl.when(tv_ref[i] == 0)
    def _():
        h_ref[...] = jnp.zeros_like(h_ref)


def _moe_down_kernel(te_ref, tv_ref, h_ref, wd_ref, w_ref, y_ref):
    i = pl.program_id(0)

    @pl.when(tv_ref[i] > 0)
    def _():
        y = jnp.dot(h_ref[...], wd_ref[...].astype(BF16), preferred_element_type=F32)
        y_ref[...] = y * w_ref[...]

    @pl.when(tv_ref[i] == 0)
    def _():
        y_ref[...] = jnp.zeros_like(y_ref)


def _moe_experts(xs, row_w, tile_expert, tile_valid, w_gate, w_up, w_down, layer):
    p, d = xs.shape
    de = w_gate.shape[-1]
    nt = p // MOE_TM
    hid = pl.pallas_call(
        _moe_up_kernel,
        grid_spec=pltpu.PrefetchScalarGridSpec(
            num_scalar_prefetch=2, grid=(nt,),
            in_specs=[pl.BlockSpec((MOE_TM, d), lambda i, te, tv: (i, 0)),
                      pl.BlockSpec((None, None, d, de), lambda i, te, tv: (layer, te[i], 0, 0)),
                      pl.BlockSpec((None, None, d, de), lambda i, te, tv: (layer, te[i], 0, 0))],
            out_specs=pl.BlockSpec((MOE_TM, de), lambda i, te, tv: (i, 0))),
        out_shape=jax.ShapeDtypeStruct((p, de), BF16),
        compiler_params=pltpu.CompilerParams(dimension_semantics=("arbitrary",)),
        name="moe_up",
    )(tile_expert, tile_valid, xs, w_gate, w_up)
    return pl.pallas_call(
        _moe_down_kernel,
        grid_spec=pltpu.PrefetchScalarGridSpec(
            num_scalar_prefetch=2, grid=(nt,),
            in_specs=[pl.BlockSpec((MOE_TM, de), lambda i, te, tv: (i, 0)),
                      pl.BlockSpec((None, None, de, d), lambda i, te, tv: (layer, te[i], 0, 0)),
                      pl.BlockSpec((MOE_TM, 1), lambda i, te, tv: (i, 0))],
            out_specs=pl.BlockSpec((MOE_TM, d), lambda i, te, tv: (i, 0))),
        out_shape=jax.ShapeDtypeStruct((p, d), F32),
        compiler_params=pltpu.CompilerParams(dimension_semantics=("arbitrary",)),
        name="moe_down",
    )(tile_expert, tile_valid, hid, w_down, row_w)


def _hier_moe(h2, layer, w_route, b_route_group, b_route_expert, w_gate, w_up, w_down, *, tm_route):
    n, d = h2.shape
    n_groups = b_route_group.shape[-1]
    n_experts = b_route_expert.shape[-1]
    logits = _mm(h2, w_route, tm=tm_route, tn=w_route.shape[-1], b_lead=(layer,))
    p_group = jax.nn.softmax(logits[:, :n_groups] + b_route_group[layer], axis=-1)
    p_top, g_idx = lax.top_k(p_group, 1)
    le = (logits[:, n_groups:n_groups + n_experts] + b_route_expert[layer])
    le = le.reshape(n, n_groups, EXP_PER_GROUP)
    le = jnp.take_along_axis(le, g_idx[:, :, None], axis=1)[:, 0]
    v_top, e_idx = lax.top_k(le, TOP_K)
    wts = p_top * jax.nn.softmax(v_top, axis=-1)
    gidx = (g_idx * EXP_PER_GROUP + e_idx).astype(jnp.int32)

    na = n * TOP_K
    nt = na // MOE_TM + n_experts
    e_flat = gidx.reshape(na)
    order = jnp.argsort(e_flat, stable=True).astype(jnp.int32)
    sorted_e = e_flat[order]
    counts = jnp.zeros((n_experts,), jnp.int32).at[e_flat].add(1)
    tiles_per = (counts + MOE_TM - 1) // MOE_TM
    tile_end = jnp.cumsum(tiles_per)
    pad_start = (tile_end - tiles_per) * MOE_TM
    count_start = jnp.cumsum(counts) - counts
    dest = pad_start[sorted_e] + jnp.arange(na, dtype=jnp.int32) - count_start[sorted_e]
    row_token = jnp.zeros((nt * MOE_TM,), jnp.int32).at[dest].set(order // TOP_K)
    row_w = jnp.zeros((nt * MOE_TM,), F32).at[dest].set(wts.reshape(na)[order])
    pos = jnp.zeros((na,), jnp.int32).at[order].set(dest)
    tile_ids = jnp.arange(nt, dtype=jnp.int32)
    tile_expert = jnp.minimum(jnp.searchsorted(tile_end, tile_ids, side='right'),
                              n_experts - 1).astype(jnp.int32)
    tile_valid = (tile_ids < tile_end[-1]).astype(jnp.int32)

    xs = jnp.take(h2, row_token, axis=0)
    ys = _moe_experts(xs, row_w[:, None], tile_expert, tile_valid, w_gate, w_up, w_down, layer)
    picked = jnp.take(ys, pos, axis=0).reshape(n, TOP_K, d)
    return picked[:, 0] + picked[:, 1]


def _rms_norm(x, g):
    return x * lax.rsqrt(jnp.mean(x * x, axis=-1, keepdims=True) + RMS_EPS) * g


def _pool_diffs(u, u_past, start_pos):
    b, t, pw = u.shape
    grp = pw // len(POOL_WINDOWS)
    ext = jnp.concatenate([u_past, u], axis=1)
    pos = start_pos + jnp.arange(t)
    diffs = []
    for gi, win in enumerate(POOL_WINDOWS):
        e = ext[:, :, gi * grp:(gi + 1) * grp]
        win_sum = e[:, POOL_BUF:POOL_BUF + t]
        for i in range(1, win):
            win_sum = win_sum + e[:, POOL_BUF - i:POOL_BUF - i + t]
        cnt = jnp.minimum(win, pos + 1).astype(F32)[None, :, None]
        diffs.append(win_sum / cnt - e[:, POOL_BUF:])
    return jnp.concatenate(diffs, axis=-1).reshape(b * t, pw), ext[:, t:]


def kernel(x_prompt, x_sample, state_shift, state_pool, state_wkv, c_prompt, c_sample, w_ada, b_ada, g_mix, g_ffn, g_final, w_in, w_vres_down, mu, mu_vres, w_pool, pool_scale, w_decay0, w_decay2, w_a0, w_a2, w_g2, w_v0, w_v2, k_k, k_a, r_k, ln_x_w, ln_x_b, w_out, w_route_group, b_route_group, w_route_expert, b_route_expert, w_exp_gate, w_exp_up, w_exp_down):
    bp, tp, d = x_prompt.shape
    bs, ts, _ = x_sample.shape
    depth = w_in.shape[0]
    n_heads = state_wkv.shape[2]
    rw = n_heads * HEAD
    pw = state_pool.shape[-1]
    n_mod = w_ada.shape[1] // d
    n_p, n_s = bp * tp, bs * ts
    n_tok = n_p + n_s
    c_main = pw + 3 * rw
    n_tail = w_in.shape[-1] - c_main
    v_lora = w_vres_down.shape[-1]
    w_lora, a_lora = w_decay2.shape[1], w_a2.shape[1]

    tm_tok = n_tok // 8
    tm_p = tp // 2
    tm_norm = math.gcd(256, tp, n_s)
    tm_prep = tm_norm
    npairs = max(q for q in range(1, WKV_PAIRS + 1)
                 if (n_heads // 2) % q == 0 and pw % (q * PAIR) == 0)

    def stream(a_p, a_s):
        return jnp.concatenate([a_p.reshape(n_p, -1), a_s.reshape(n_s, -1)], axis=0)

    c_all = jnp.concatenate([c_prompt, c_sample], axis=0)
    rows_c = -(-c_all.shape[0] // 16) * 16
    c_act = jnp.pad(jax.nn.silu(c_all), ((0, rows_c - c_all.shape[0]), (0, 0))).astype(BF16)
    mod = _mm(c_act, w_ada, tm=rows_c, tn=512)[:bp + bs].reshape(bp + bs, n_mod, d)

    def per_token(vec):
        return stream(jnp.broadcast_to(vec[:bp, None], (bp, tp, d)),
                      jnp.broadcast_to(vec[bp:, None], (bs, ts, d)))

    zeros_l = lambda *shape: jnp.zeros((1,) + shape, F32)
    w_tail = jnp.concatenate(
        [w_in[:, :, c_main:], jnp.concatenate([zeros_l(d, v_lora), w_vres_down], axis=0)], axis=-1)
    mu_main = jnp.concatenate([jnp.zeros((depth, pw), F32), mu[:, :3 * rw]], axis=-1)[:, None]
    mu_tail = jnp.concatenate(
        [mu[:, 3 * rw:], jnp.concatenate([zeros_l(v_lora), mu_vres], axis=0)], axis=-1)[:, None]
    route_pad = 128 - (w_route_group.shape[-1] + w_route_expert.shape[-1])
    w_route = jnp.concatenate(
        [w_route_group, w_route_expert, jnp.zeros((depth, d, route_pad), F32)], axis=-1)
    pool_scale3 = pool_scale[:, None]

    lora_weights = (w_decay2, w_a2, w_g2, w_v2)
    lora_vecs = (w_decay0[:, None], w_a0[:, None], k_k[:, None], k_a[:, None], w_v0[:, None])
    assert rw % pw == 0 and pw % (npairs * PAIR) == 0

    def mixer_inputs(z, zt, v_first, l, tm):
        acts = (jnp.tanh(zt[:, :w_lora]).astype(BF16),
                zt[:, w_lora:w_lora + a_lora].astype(BF16),
                jax.nn.sigmoid(zt[:, w_lora + a_lora:n_tail]).astype(BF16),
                zt[:, n_tail:].astype(BF16))
        return _prep(z, acts, v_first, lora_weights, lora_vecs, l, tm=tm, wc=pw,
                     k_off=1 + rw // pw, v_off=1 + 2 * (rw // pw))

    x = stream(x_prompt, x_sample)
    vf_p = vf_s = None
    shift_p, pool_p, wkv_p, shift_s, pool_s, wkv_s = [], [], [], [], [], []
    zero_state = jnp.zeros((bp, n_heads, HEAD, HEAD), F32)
    for l in range(depth):
        m = mod + b_ada[l]
        m_seq = m[:bp].transpose(1, 0, 2)[:, :, None]
        m_tok = jnp.repeat(m[bp:], ts, axis=0).transpose(1, 0, 2)
        x_last = jnp.concatenate([x[:n_p].reshape(bp, tp, d)[:, -1], x[n_p:].reshape(bs, ts, d)[:, -1]])
        h_last = _rms_norm(x_last, g_mix[l]) * (1 + m[:, 1]) + m[:, 0]
        shift_p.append(h_last[:bp])
        shift_s.append(h_last[bp:])
        h_bf = _norm_mod(x, g_mix[l], m_seq, m_tok, 1, 0, tm=tm_norm, rows_per_seq=tp)
        hs_bf = jnp.concatenate([state_shift[l][:, None].astype(BF16), h_bf[n_p:].reshape(bs, ts, d)],
                                axis=1).reshape(bs * (ts + 1), d)

        z_p = _mm_lerp(h_bf, w_in, mu_main, l, tm=tm_p, tn=512, m=n_p, n=c_main,
                       tiles_per_seq=tp // tm_p)
        zt_p = _mm_lerp(h_bf, w_tail, mu_tail, l, tm=tm_p, tn=n_tail + v_lora, m=n_p,
                        n=n_tail + v_lora, tiles_per_seq=tp // tm_p)
        drop_first = lambda a_: a_.reshape(bs, ts + 1, -1)[:, 1:].reshape(n_s, -1)
        z_s = drop_first(_mm_lerp(hs_bf, w_in, mu_main, l, tm=hs_bf.shape[0], tn=512,
                                  m=hs_bf.shape[0], n=c_main, tiles_per_seq=None))
        zt_s = drop_first(_mm_lerp(hs_bf, w_tail, mu_tail, l, tm=hs_bf.shape[0],
                                   tn=n_tail + v_lora, m=hs_bf.shape[0], n=n_tail + v_lora,
                                   tiles_per_seq=None))

        d_p, np_p = _pool_diffs(z_p[:, :pw].reshape(bp, tp, pw), jnp.zeros((bp, POOL_BUF, pw), F32), 0)
        d_s, np_s = _pool_diffs(z_s[:, :pw].reshape(bs, ts, pw), state_pool[l], PAST_LEN)
        pool_p.append(np_p)
        pool_s.append(np_s)
        pool_out = _pool_mm(jnp.concatenate([d_p, d_s], axis=0).astype(BF16), w_pool, pool_scale3,
                            l, tm=tm_tok)

        in_p = mixer_inputs(z_p, zt_p, vf_p, l, tm_prep)
        in_s = mixer_inputs(z_s, zt_s, vf_s, l, tm_prep)
        if l == 0:
            vf_p, vf_s = in_p[2], in_s[2]
        seq_p = lambda a_: a_.reshape(bp, tp, -1)
        seq_s = lambda a_: a_.reshape(bs, ts, rw)
        o_p, s_p = _wkv_chunked(seq_p(z_p), *[seq_p(a_) for a_ in in_p], r_k[l], ln_x_w[l], ln_x_b[l],
                                zero_state, npairs=npairs, r_off=pw)
        ld_s, k_s, v_s, kk_s, kka_s, g_s = in_s
        r_s = z_s[:, pw:pw + rw]
        o_s, s_s = _wkv(seq_s(r_s), seq_s(jnp.exp(ld_s)), seq_s(k_s), seq_s(v_s), seq_s(kk_s),
                        seq_s(kka_s), state_wkv[l], tc=ts)
        wkv_p.append(s_p)
        wkv_s.append(s_s)
        heads = lambda a_: a_.reshape(n_s, n_heads, HEAD)
        o_s = heads(o_s)
        mean = jnp.mean(o_s, axis=-1, keepdims=True)
        var = jnp.mean(jnp.square(o_s - mean), axis=-1, keepdims=True)
        o_s = ((o_s - mean) * lax.rsqrt(var + GN_EPS)).reshape(n_s, rw) * ln_x_w[l] + ln_x_b[l]
        bonus = jnp.sum(heads(r_s) * heads(k_s) * r_k[l], axis=-1, keepdims=True) * heads(v_s)
        o_s = ((o_s + bonus.reshape(n_s, rw)) * g_s).astype(BF16)
        o = jnp.concatenate([o_p.reshape(n_p, rw), o_s], axis=0)
        x = _mix_out(pool_out, o, w_out, x, per_token(m[:, 2]), l, tm=tm_tok, tn=512)

        h2 = _norm_mod(x, g_ffn[l], m_seq, m_tok, 4, 3, tm=tm_norm, rows_per_seq=tp)
        moe = _hier_moe(h2, l, w_route, b_route_group, b_route_expert,
                        w_exp_gate, w_exp_up, w_exp_down, tm_route=tm_tok)
        x = x + per_token(m[:, 5]) * moe

    y = _rms_norm(x, g_final)
    return (y[:n_p].reshape(bp, tp, d), y[n_p:].reshape(bs, ts, d),
            jnp.stack(shift_p), jnp.stack(pool_p), jnp.stack(wkv_p),
            jnp.stack(shift_s), jnp.stack(pool_s), jnp.stack(wkv_s))
```

```python
import functools
import math

import jax
import jax.numpy as jnp
from jax import lax
from jax.experimental import pallas as pl
from jax.experimental.pallas import tpu as pltpu

F32 = jnp.float32
BF16 = jnp.bfloat16

HEAD = 64
PAIR = 2 * HEAD
SUBLANES = 8
POOL_WINDOWS = (2, 4, 8, 16)
POOL_BUF = max(POOL_WINDOWS) - 1
EXP_PER_GROUP = 8
TOP_K = 2
RMS_EPS = 1e-6
GN_EPS = 64e-5
MOE_TM = 256
PAST_LEN = 16384
WKV_PAIRS = 8


def _dot_bf16(a_ref, b_ref):
    return jnp.dot(a_ref[...].astype(BF16), b_ref[...].astype(BF16), preferred_element_type=F32)


def _mm_kernel(a_ref, b_ref, o_ref):
    o_ref[...] = _dot_bf16(a_ref, b_ref).astype(o_ref.dtype)


def _mm(a, b, *, tm, tn, b_lead=(), n=None, m=None, out_dtype=F32):
    k = a.shape[1]
    m = a.shape[0] if m is None else m
    n = b.shape[-1] if n is None else n
    assert b.shape[-2] == k and m % tm == 0 and n % tn == 0, (a.shape, b.shape, m, tm, n, tn)
    lead = tuple(b_lead)
    return pl.pallas_call(
        _mm_kernel,
        grid=(m // tm, n // tn),
        in_specs=[pl.BlockSpec((tm, k), lambda i, j: (i, 0)),
                  pl.BlockSpec((None,) * len(lead) + (k, tn), lambda i, j: lead + (0, j))],
        out_specs=pl.BlockSpec((tm, tn), lambda i, j: (i, j)),
        out_shape=jax.ShapeDtypeStruct((m, n), out_dtype),
        compiler_params=pltpu.CompilerParams(dimension_semantics=("parallel", "arbitrary")),
        name="mm",
    )(a, b)


def _mm_lerp_kernel(a_ref, b_ref, mu_ref, o_ref, carry_ref, *, tiles_per_seq):
    i, j = pl.program_id(0), pl.program_id(1)
    acc = _dot_bf16(a_ref, b_ref)
    prev = pltpu.roll(acc, 1, axis=0)
    if tiles_per_seq is not None:
        @pl.when(i == 0)
        def _():
            carry_ref[j] = jnp.zeros(carry_ref.shape[1:], F32)

        row = lax.broadcasted_iota(jnp.int32, acc.shape, 0)
        last = jnp.where(i % tiles_per_seq == 0, 0., carry_ref[j][SUBLANES - 1:])
        prev = jnp.where(row == 0, last, prev)
        carry_ref[j] = acc[acc.shape[0] - SUBLANES:]
    o_ref[...] = acc + (prev - acc) * mu_ref[...]


def _mm_lerp(a, b, mu_cols, layer, *, tm, tn, m, n, tiles_per_seq):
    k = a.shape[1]
    assert m % tm == 0 and n % tn == 0
    return pl.pallas_call(
        functools.partial(_mm_lerp_kernel, tiles_per_seq=tiles_per_seq),
        grid=(m // tm, n // tn),
        in_specs=[pl.BlockSpec((tm, k), lambda i, j: (i, 0)),
                  pl.BlockSpec((None, k, tn), lambda i, j: (layer, 0, j)),
                  pl.BlockSpec((None, 1, tn), lambda i, j: (layer, 0, j))],
        out_specs=pl.BlockSpec((tm, tn), lambda i, j: (i, j)),
        out_shape=jax.ShapeDtypeStruct((m, n), F32),
        scratch_shapes=[pltpu.VMEM((n // tn, SUBLANES, tn), F32)],
        compiler_params=pltpu.CompilerParams(dimension_semantics=("arbitrary", "arbitrary")),
        name="proj_lerp",
    )(a, b, mu_cols)


def _pool_mm_kernel(a_ref, b_ref, s_ref, o_ref):
    o_ref[...] = (_dot_bf16(a_ref, b_ref) * s_ref[...]).astype(o_ref.dtype)


def _pool_mm(d, w_pool, scale, layer, *, tm):
    m = d.shape[0]
    g, c = w_pool.shape[1], w_pool.shape[2]
    return pl.pallas_call(
        _pool_mm_kernel,
        grid=(m // tm, g),
        in_specs=[pl.BlockSpec((tm, c), lambda i, j: (i, j)),
                  pl.BlockSpec((None, None, c, c), lambda i, j: (layer, j, 0, 0)),
                  pl.BlockSpec((None, 1, c), lambda i, j: (layer, 0, j))],
        out_specs=pl.BlockSpec((tm, c), lambda i, j: (i, j)),
        out_shape=jax.ShapeDtypeStruct((m, g * c), BF16),
        compiler_params=pltpu.CompilerParams(dimension_semantics=("parallel", "arbitrary")),
        name="pool_mm",
    )(d, w_pool, scale)


class _RowMod:
    def __init__(self, m_seq, m_tok, rows_per_seq, tm):
        self.m_seq, self.m_tok, self.tm = m_seq, m_tok, tm
        self.n_seq = m_seq.shape[1]
        assert rows_per_seq % tm == 0 and m_tok.shape[1] % tm == 0
        self.tiles_per_seq = rows_per_seq // tm
        self.seq_tiles = self.n_seq * self.tiles_per_seq
        self.n_rows = self.n_seq * rows_per_seq + m_tok.shape[1]

    def specs(self, comp, tn, row_col):
        def seq_map(*ids):
            i, j = row_col(*ids)
            return comp, jnp.minimum(i // self.tiles_per_seq, self.n_seq - 1), 0, j

        def tok_map(*ids):
            i, j = row_col(*ids)
            return comp, jnp.maximum(i - self.seq_tiles, 0), j

        return [pl.BlockSpec((None, None, 1, tn), seq_map), pl.BlockSpec((None, self.tm, tn), tok_map)]

    def pick(self, row_tile, seq_ref, tok_ref):
        return jnp.where(row_tile < self.seq_tiles, seq_ref[...], tok_ref[...])


def _mix_out_kernel(p_ref, o1_ref, o2_ref, o3_ref, w0_ref, w1_ref, w2_ref, w3_ref, x_ref,
                    g_seq_ref, g_tok_ref, y_ref, *, mod):
    acc = (_dot_bf16(p_ref, w0_ref) + _dot_bf16(o1_ref, w1_ref)
           + _dot_bf16(o2_ref, w2_ref) + _dot_bf16(o3_ref, w3_ref))
    y_ref[...] = x_ref[...] + mod.pick(pl.program_id(1), g_seq_ref, g_tok_ref) * acc


def _mix_out(pool_out, o, w_out, x, mod, gate_i, layer, *, tn):
    n, d = x.shape
    tm = mod.tm
    q = pool_out.shape[1]
    assert o.shape[1] == 3 * q and d == 4 * q and n == mod.n_rows
    w4 = w_out.reshape(w_out.shape[0], 4, q, d)
    a_spec = lambda c: pl.BlockSpec((tm, q), lambda j, i: (i, c))
    w_spec = lambda c: pl.BlockSpec((None, None, q, tn), lambda j, i: (layer, c, 0, j))
    tile = pl.BlockSpec((tm, tn), lambda j, i: (i, j))
    return pl.pallas_call(
        functools.partial(_mix_out_kernel, mod=mod),
        grid=(d // tn, n // tm),
        in_specs=[a_spec(0), a_spec(0), a_spec(1), a_spec(2),
                  w_spec(0), w_spec(1), w_spec(2), w_spec(3), tile]
        + mod.specs(gate_i, tn, lambda j, i: (i, j)),
        out_specs=tile,
        out_shape=jax.ShapeDtypeStruct((n, d), F32),
        compiler_params=pltpu.CompilerParams(dimension_semantics=("parallel", "arbitrary")),
        name="mix_out",
    )(pool_out, o, o, o, w4, w4, w4, w4, x, mod.m_seq, mod.m_tok)


def _norm_mod_kernel(x_ref, g_ref, sc_seq_ref, sc_tok_ref, sh_seq_ref, sh_tok_ref, o_ref, *, mod):
    x = x_ref[...]
    y = x * lax.rsqrt(jnp.mean(x * x, axis=-1, keepdims=True) + RMS_EPS) * g_ref[...]
    i = pl.program_id(0)
    scale = mod.pick(i, sc_seq_ref, sc_tok_ref)
    shift = mod.pick(i, sh_seq_ref, sh_tok_ref)
    o_ref[...] = (y * (1 + scale) + shift).astype(o_ref.dtype)


def _norm_mod(x, g, mod, scale_i, shift_i):
    n, d = x.shape
    tm = mod.tm
    assert n == mod.n_rows
    row_col = lambda i: (i, 0)
    return pl.pallas_call(
        functools.partial(_norm_mod_kernel, mod=mod),
        grid=(n // tm,),
        in_specs=[pl.BlockSpec((tm, d), lambda i: (i, 0)), pl.BlockSpec((1, d), lambda i: (0, 0))]
        + mod.specs(scale_i, d, row_col) + mod.specs(shift_i, d, row_col),
        out_specs=pl.BlockSpec((tm, d), lambda i: (i, 0)),
        out_shape=jax.ShapeDtypeStruct((n, d), BF16),
        compiler_params=pltpu.CompilerParams(dimension_semantics=("parallel",)),
        name="norm_mod",
    )(x, g.reshape(1, d), mod.m_seq, mod.m_tok, mod.m_seq, mod.m_tok)


def _moe_combine_kernel(pos_ref, x_ref, g_seq_ref, g_tok_ref, ys_ref, o_ref, buf_ref, sem, *, mod):
    tm = mod.tm
    i = pl.program_id(0)
    base = i * (tm * TOP_K)

    def row_copy(t, k):
        return pltpu.make_async_copy(ys_ref.at[pl.ds(pos_ref[base + t * TOP_K + k], 1)],
                                     buf_ref.at[k, pl.ds(t, 1)], sem)

    def issue(t, carry):
        for k in range(TOP_K):
            row_copy(t, k).start()
        return carry

    def drain(t, carry):
        for k in range(TOP_K):
            row_copy(t, k).wait()
        return carry

    lax.fori_loop(0, tm, issue, 0)
    lax.fori_loop(0, tm, drain, 0)
    moe = buf_ref[0]
    for k in range(1, TOP_K):
        moe = moe + buf_ref[k]
    o_ref[...] = x_ref[...] + mod.pick(i, g_seq_ref, g_tok_ref) * moe


def _moe_combine(x, ys, pos, mod, gate_i):
    n, d = x.shape
    tm = mod.tm
    assert n == mod.n_rows and pos.shape == (n * TOP_K,)
    tile = pl.BlockSpec((tm, d), lambda i, pos_ref: (i, 0))
    return pl.pallas_call(
        functools.partial(_moe_combine_kernel, mod=mod),
        grid_spec=pltpu.PrefetchScalarGridSpec(
            num_scalar_prefetch=1, grid=(n // tm,),
            in_specs=[tile] + mod.specs(gate_i, d, lambda i, pos_ref: (i, 0))
            + [pl.BlockSpec(memory_space=pl.ANY)],
            out_specs=tile,
            scratch_shapes=[pltpu.VMEM((TOP_K, tm, d), F32), pltpu.SemaphoreType.DMA(())]),
        out_shape=jax.ShapeDtypeStruct((n, d), F32),
        compiler_params=pltpu.CompilerParams(dimension_semantics=("arbitrary",)),
        name="moe_combine",
    )(pos, x, mod.m_seq, mod.m_tok, ys)


def _seg_sum(x, ones_blockdiag):
    hi = x.astype(BF16)
    lo = (x - hi.astype(F32)).astype(BF16)
    return (jnp.dot(hi, ones_blockdiag, preferred_element_type=F32)
            + jnp.dot(lo, ones_blockdiag, preferred_element_type=F32))


def _ones_blockdiag():
    same_head = ((lax.broadcasted_iota(jnp.int32, (PAIR, PAIR), 0) >= HEAD)
                 == (lax.broadcasted_iota(jnp.int32, (PAIR, PAIR), 1) >= HEAD))
    return same_head.astype(BF16)


def _prep_kernel(*refs, first_layer):
    if first_layer:
        (k_ref, v_ref, twd_ref, ad_ref, sgd_ref, wd2_ref, wa2_ref, wg2_ref, d0_ref, a0_ref, kk_w_ref,
         ka_ref, ld_o, kmod_o, v_o, kk_o, kka_o, g_o) = refs
    else:
        (k_ref, v_ref, twd_ref, ad_ref, sgd_ref, wd2_ref, wa2_ref, wg2_ref, d0_ref, a0_ref, kk_w_ref,
         ka_ref, vf_ref, vd_ref, wv2_ref, v0_ref, ld_o, kmod_o, v_o, kk_o, kka_o, g_o) = refs
    k = k_ref[...]
    y = -(d0_ref[...] + _dot_bf16(twd_ref, wd2_ref))
    softplus = jnp.maximum(y, 0.) + jnp.log(1. + jnp.exp(-jnp.abs(y)))
    ld_o[...] = -jnp.exp(-softplus - 0.5)
    a = jax.nn.sigmoid(a0_ref[...] + _dot_bf16(ad_ref, wa2_ref))
    g_o[...] = _dot_bf16(sgd_ref, wg2_ref)
    v = v_ref[...]
    if not first_layer:
        v = v + (vf_ref[...] - v) * jax.nn.sigmoid(v0_ref[...] + _dot_bf16(vd_ref, wv2_ref))
    v_o[...] = v
    kk = k * kk_w_ref[...]
    tm, wc = kk.shape
    nblk = wc // PAIR
    sq = jnp.concatenate([(kk * kk)[:, q * PAIR:(q + 1) * PAIR] for q in range(nblk)], axis=0)
    ss = _seg_sum(sq, _ones_blockdiag())
    ss = jnp.concatenate([ss[q * tm:(q + 1) * tm] for q in range(nblk)], axis=1)
    kk = kk / jnp.maximum(jnp.sqrt(ss), 1e-12)
    kk_o[...] = kk
    kka_o[...] = kk * a
    kmod_o[...] = k * (1 + (a - 1) * ka_ref[...])


def _prep(z, acts, v_first, weights, vecs, layer, *, tm, wc, k_off, v_off):
    n = z.shape[0]
    twd, ad, sgd, vd = acts
    w_d2, w_a2, w_g2, w_v2 = weights
    d0, a0, kk_w, ka, v0 = vecs
    rw = w_d2.shape[-1]
    first_layer = v_first is None
    col = lambda off: pl.BlockSpec((tm, wc), lambda i, j: (i, j + off))
    act = lambda a_: pl.BlockSpec((tm, a_.shape[1]), lambda i, j: (i, 0))
    lw = lambda w_, l_: pl.BlockSpec((None, w_.shape[1], wc), lambda i, j: (l_, 0, j))
    vec = lambda l_: pl.BlockSpec((None, 1, wc), lambda i, j: (l_, 0, j))
    args = [z, z, twd, ad, sgd, w_d2, w_a2, w_g2, d0, a0, kk_w, ka]
    specs = [col(k_off), col(v_off), act(twd), act(ad), act(sgd), lw(w_d2, layer), lw(w_a2, layer),
             lw(w_g2, layer), vec(layer), vec(layer), vec(layer), vec(layer)]
    if not first_layer:
        args += [v_first, vd, w_v2, v0]
        specs += [col(0), act(vd), lw(w_v2, layer - 1), vec(layer - 1)]
    out = jax.ShapeDtypeStruct((n, rw), F32)
    return pl.pallas_call(
        functools.partial(_prep_kernel, first_layer=first_layer),
        grid=(n // tm, rw // wc),
        in_specs=specs,
        out_specs=[col(0)] * 6,
        out_shape=[out] * 6,
        compiler_params=pltpu.CompilerParams(dimension_semantics=("parallel", "arbitrary")),
        name="rwkv_prep",
    )(*args)


def _wkv_kernel(r_ref, d_ref, k_ref, v_ref, kk_ref, kka_ref, s0_ref, o_ref, s_ref, *, tc, npairs):
    @pl.when(pl.program_id(1) == 0)
    def _():
        s_ref[...] = s0_ref[...]

    lane = lax.broadcasted_iota(jnp.int32, (HEAD, PAIR), 1)
    row = lax.broadcasted_iota(jnp.int32, (HEAD, PAIR), 0)
    diag = ((lane & (HEAD - 1)) == row).astype(F32)
    jr = lax.broadcasted_iota(jnp.int32, (PAIR, PAIR), 0) >= HEAD
    jc = lax.broadcasted_iota(jnp.int32, (PAIR, PAIR), 1) >= HEAD
    ones_blockdiag = (jr == jc).astype(BF16)

    rows = min(tc, SUBLANES)
    row_id = lax.broadcasted_iota(jnp.int32, (rows, PAIR), 0)

    def row_group(gi, carry):
        base = pl.multiple_of(gi * rows, rows)

        def rowvec(ref, p, i):
            return ref[0, pl.ds(base, rows), p * PAIR:(p + 1) * PAIR][i:i + 1]

        o_tiles = [jnp.zeros((rows, PAIR), F32)] * npairs
        for i in range(rows):
            xs = []
            for p in range(npairs):
                xs.append(s_ref[0, p] * rowvec(kk_ref, p, i))
                xs.append(diag * rowvec(v_ref, p, i))
            y = _seg_sum(jnp.concatenate(xs, axis=0), ones_blockdiag)
            qs = []
            for p in range(npairs):
                s_kk = y[p * PAIR:p * PAIR + HEAD]
                v_col = y[p * PAIR + HEAD:(p + 1) * PAIR]
                s = (s_ref[0, p] * rowvec(d_ref, p, i) - s_kk * rowvec(kka_ref, p, i)
                     + v_col * rowvec(k_ref, p, i))
                s_ref[0, p] = s
                qs.append(s * rowvec(r_ref, p, i))
            z = _seg_sum(jnp.concatenate(qs, axis=0), ones_blockdiag)
            for p in range(npairs):
                o_col = z[p * HEAD:(p + 1) * HEAD]
                o_row = jnp.sum(o_col * diag, axis=0, keepdims=True)
                o_tiles[p] = jnp.where(row_id == i, o_row, o_tiles[p])
        for p in range(npairs):
            o_ref[0, pl.ds(base, rows), p * PAIR:(p + 1) * PAIR] = o_tiles[p]
        return carry

    lax.fori_loop(0, tc // rows, row_group, 0)


def _pair_state(s):
    b, h = s.shape[:2]
    return s.reshape(b, h // 2, 2, HEAD, HEAD).transpose(0, 1, 3, 2, 4).reshape(b, h // 2, HEAD, PAIR)


def _unpair_state(s):
    b, hp = s.shape[:2]
    return s.reshape(b, hp, HEAD, 2, HEAD).transpose(0, 1, 3, 2, 4).reshape(b, 2 * hp, HEAD, HEAD)


def _wkv(r, decay, k, v, kk, kka, s0, *, tc):
    b, t, w = r.shape
    npairs = w // PAIR
    assert t % tc == 0
    seq = pl.BlockSpec((1, tc, w), lambda i, c: (i, c, 0))
    st = pl.BlockSpec((1, npairs, HEAD, PAIR), lambda i, c: (i, 0, 0, 0))
    o, s = pl.pallas_call(
        functools.partial(_wkv_kernel, tc=tc, npairs=npairs),
        grid=(b, t // tc),
        in_specs=[seq] * 6 + [st],
        out_specs=[seq, st],
        out_shape=[jax.ShapeDtypeStruct((b, t, w), F32),
                   jax.ShapeDtypeStruct((b, npairs, HEAD, PAIR), F32)],
        compiler_params=pltpu.CompilerParams(dimension_semantics=("parallel", "arbitrary")),
        name="wkv_steps",
    )(r, decay, k, v, kk, kka, _pair_state(s0))
    return o, _unpair_state(s)


def _bdot(x, y):
    return jnp.dot(x.astype(BF16), y.astype(BF16), preferred_element_type=F32)


def _bdot_nt(x, y):
    return lax.dot_general(x.astype(BF16), y.astype(BF16), (((1,), (1,)), ((), ())),
                           preferred_element_type=F32)


def _split3(x):
    h1 = x.astype(BF16)
    r1 = x - h1.astype(F32)
    h2 = r1.astype(BF16)
    return h1, h2, (r1 - h2.astype(F32)).astype(BF16)


def _wkv_chunk_kernel(r_ref, ld_ref, k_ref, v_ref, kk_ref, kka_ref, g_ref, rk_ref, lnw_ref, lnb_ref,
                      s0_ref, o_ref, s_ref, *, npairs):
    c = HEAD

    @pl.when(pl.program_id(2) == 0)
    def _():
        s_ref[...] = s0_ref[...]

    row = lax.broadcasted_iota(jnp.int32, (c, PAIR), 0)
    col = lax.broadcasted_iota(jnp.int32, (c, PAIR), 1)
    left = col < c
    strict = (col & (c - 1)) < row
    incl = (col & (c - 1)) <= row
    tri = (lax.broadcasted_iota(jnp.int32, (c, c), 1)
           <= lax.broadcasted_iota(jnp.int32, (c, c), 0)).astype(BF16)
    same_head = ((lax.broadcasted_iota(jnp.int32, (PAIR, PAIR), 0) >= HEAD)
                 == (lax.broadcasted_iota(jnp.int32, (PAIR, PAIR), 1) >= HEAD))
    ones_blockdiag = same_head.astype(BF16)

    def halves(x):
        return jnp.concatenate([jnp.where(left, x, 0.), jnp.where(left, 0., x)], axis=1)

    def pairs(x):
        return jnp.stack([x[:, p * PAIR:(p + 1) * PAIR] for p in range(npairs)])

    def bmm(x, y):
        return lax.dot_general(x.astype(BF16), y.astype(BF16), (((2,), (1,)), ((0,), (0,))),
                               preferred_element_type=F32)

    def bmm_nt(x, y):
        return lax.dot_general(x.astype(BF16), y.astype(BF16), (((2,), (2,)), ((0,), (0,))),
                               preferred_element_type=F32)

    def seg_sum(x):
        return _seg_sum(x.reshape(npairs * c, PAIR), ones_blockdiag).reshape(npairs, c, PAIR)

    ld = ld_ref[0]
    r, k, v = r_ref[0], k_ref[0], v_ref[0]
    b = sum(jnp.dot(tri, part, preferred_element_type=F32) for part in _split3(ld))
    eb = jnp.exp(b)
    e_inv = jnp.exp(-b)
    a_t = pairs(-kk_ref[0] * jnp.exp(b - ld))
    r_t = pairs(r * eb)
    k_t = pairs(k * e_inv)
    b_t = pairs(kka_ref[0] * e_inv)
    eb_last = pairs(eb[c - 1:c])
    v_p = pairs(v)
    s = s_ref[0]

    gram = bmm_nt(jnp.concatenate([a_t, r_t], axis=1),
                  jnp.concatenate([halves(b_t), halves(k_t)], axis=1))
    l_ab = jnp.where(strict, gram[:, :c, :PAIR], 0.)
    l_ak = jnp.where(strict, gram[:, :c, PAIR:], 0.)
    m_rb = jnp.where(incl, gram[:, c:, :PAIR], 0.)
    m_rk = jnp.where(incl, gram[:, c:, PAIR:], 0.)

    v_h = halves(v_p)
    u = bmm_nt(a_t, s) + bmm(l_ak, v_h)
    l_pow = l_ab
    n = 1
    while True:
        u = u + bmm(l_pow, halves(u))
        n *= 2
        if n >= c:
            break
        l_pow = bmm(l_pow, halves(l_pow))
    o = bmm_nt(r_t, s) + bmm(jnp.concatenate([m_rk, m_rb], axis=2),
                             jnp.concatenate([v_h, halves(u)], axis=1))
    vu = jnp.concatenate([v_p, u], axis=1)
    vu_t = jnp.stack([vu[p].T for p in range(npairs)])
    s_new = bmm(vu_t, jnp.concatenate([k_t, b_t], axis=1))
    s_ref[0] = (s + jnp.where(same_head, s_new, 0.)) * eb_last

    mean = seg_sum(o) * (1.0 / HEAD)
    dev = o - mean
    var = seg_sum(dev * dev) * (1.0 / HEAD)
    bonus = seg_sum(pairs(r * k * rk_ref[...])) * v_p
    o = dev * lax.rsqrt(var + GN_EPS) * pairs(lnw_ref[...]) + pairs(lnb_ref[...]) + bonus
    o = o * pairs(g_ref[0])
    for p in range(npairs):
        o_ref[0, :, p * PAIR:(p + 1) * PAIR] = o[p].astype(o_ref.dtype)


def _blockdiag_state(s):
    b, h = s.shape[:2]
    s = s.reshape(b, h // 2, 2, HEAD, HEAD)
    z = jnp.zeros_like(s[:, :, 0])
    return jnp.concatenate([jnp.concatenate([s[:, :, 0], z], axis=-1),
                            jnp.concatenate([z, s[:, :, 1]], axis=-1)], axis=-2)


def _unblockdiag_state(s):
    b, hp = s.shape[:2]
    return jnp.stack([s[:, :, :HEAD, :HEAD], s[:, :, HEAD:, HEAD:]], axis=2).reshape(
        b, 2 * hp, HEAD, HEAD)


def _wkv_chunked(r, log_decay, k, v, kk, kka, g, r_k, ln_w, ln_b, s0, *, npairs, r_off=0):
    b, t, w = log_decay.shape
    wb = npairs * PAIR
    assert t % HEAD == 0 and (w // PAIR) % npairs == 0 and r_off % wb == 0
    seq = pl.BlockSpec((1, HEAD, wb), lambda i, j, c: (i, c, j))
    r_seq = pl.BlockSpec((1, HEAD, wb), lambda i, j, c: (i, c, j + r_off // wb))
    vec = pl.BlockSpec((1, wb), lambda i, j, c: (0, j))
    st = pl.BlockSpec((1, npairs, PAIR, PAIR), lambda i, j, c: (i, j, 0, 0))
    o, s = pl.pallas_call(
        functools.partial(_wkv_chunk_kernel, npairs=npairs),
        grid=(b, w // wb, t // HEAD),
        in_specs=[r_seq] + [seq] * 6 + [vec] * 3 + [st],
        out_specs=[seq, st],
        out_shape=[jax.ShapeDtypeStruct((b, t, w), BF16),
                   jax.ShapeDtypeStruct((b, w // PAIR, PAIR, PAIR), F32)],
        compiler_params=pltpu.CompilerParams(
            dimension_semantics=("parallel", "parallel", "arbitrary")),
        name="wkv_chunked",
    )(r, log_decay, k, v, kk, kka, g, r_k.reshape(1, w), ln_w.reshape(1, w), ln_b.reshape(1, w),
      _blockdiag_state(s0))
    return o, _unblockdiag_state(s)


def _moe_up_kernel(te_ref, tv_ref, x_ref, wg_ref, wu_ref, h_ref):
    i = pl.program_id(0)

    @pl.when(tv_ref[i] > 0)
    def _():
        x = x_ref[...]
        g = jnp.dot(x, wg_ref[...].astype(BF16), preferred_element_type=F32)
        u = jnp.dot(x, wu_ref[...].astype(BF16), preferred_element_type=F32)
        h_ref[...] = (g * jax.nn.sigmoid(g) * u).astype(h_ref.dtype)

    @pl.when(tv_ref[i] == 0)
    def _():
        h_ref[...] = jnp.zeros_like(h_ref)


def _moe_down_kernel(te_ref, tv_ref, h_ref, wd_ref, w_ref, y_ref):
    i = pl.program_id(0)

    @pl.when(tv_ref[i] > 0)
    def _():
        y = jnp.dot(h_ref[...], wd_ref[...].astype(BF16), preferred_element_type=F32)
        y_ref[...] = y * w_ref[...]

    @pl.when(tv_ref[i] == 0)
    def _():
        y_ref[...] = jnp.zeros_like(y_ref)


def _moe_experts(xs, row_w, tile_expert, tile_valid, w_gate, w_up, w_down, layer):
    p, d = xs.shape
    de = w_gate.shape[-1]
    nt = p // MOE_TM
    hid = pl.pallas_call(
        _moe_up_kernel,
        grid_spec=pltpu.PrefetchScalarGridSpec(
            num_scalar_prefetch=2, grid=(nt,),
            in_specs=[pl.BlockSpec((MOE_TM, d), lambda i, te, tv: (i, 0)),
                      pl.BlockSpec((None, None, d, de), lambda i, te, tv: (layer, te[i], 0, 0)),
                      pl.BlockSpec((None, None, d, de), lambda i, te, tv: (layer, te[i], 0, 0))],
            out_specs=pl.BlockSpec((MOE_TM, de), lambda i, te, tv: (i, 0))),
        out_shape=jax.ShapeDtypeStruct((p, de), BF16),
        compiler_params=pltpu.CompilerParams(dimension_semantics=("arbitrary",)),
        name="moe_up",
    )(tile_expert, tile_valid, xs, w_gate, w_up)
    return pl.pallas_call(
        _moe_down_kernel,
        grid_spec=pltpu.PrefetchScalarGridSpec(
            num_scalar_prefetch=2, grid=(nt,),
            in_specs=[pl.BlockSpec((MOE_TM, de), lambda i, te, tv: (i, 0)),
                      pl.BlockSpec((None, None, de, d), lambda i, te, tv: (layer, te[i], 0, 0)),
                      pl.BlockSpec((MOE_TM, 1), lambda i, te, tv: (i, 0))],
            out_specs=pl.BlockSpec((MOE_TM, d), lambda i, te, tv: (i, 0))),
        out_shape=jax.ShapeDtypeStruct((p, d), F32),
        compiler_params=pltpu.CompilerParams(dimension_semantics=("arbitrary",)),
        name="moe_down",
    )(tile_expert, tile_valid, hid, w_down, row_w)


def _hier_moe(h2, layer, w_route, b_route_group, b_route_expert, w_gate, w_up, w_down, *, tm_route):
    n, d = h2.shape
    n_groups = b_route_group.shape[-1]
    n_experts = b_route_expert.shape[-1]
    logits = _mm(h2, w_route, tm=tm_route, tn=w_route.shape[-1], b_lead=(layer,))
    p_group = jax.nn.softmax(logits[:, :n_groups] + b_route_group[layer], axis=-1)
    p_top, g_idx = lax.top_k(p_group, 1)
    le = (logits[:, n_groups:n_groups + n_experts] + b_route_expert[layer])
    le = le.reshape(n, n_groups, EXP_PER_GROUP)
    le = jnp.take_along_axis(le, g_idx[:, :, None], axis=1)[:, 0]
    v_top, e_idx = lax.top_k(le, TOP_K)
    wts = p_top * jax.nn.softmax(v_top, axis=-1)
    gidx = (g_idx * EXP_PER_GROUP + e_idx).astype(jnp.int32)

    na = n * TOP_K
    nt = na // MOE_TM + n_experts
    e_flat = gidx.reshape(na)
    order = jnp.argsort(e_flat, stable=True).astype(jnp.int32)
    sorted_e = e_flat[order]
    counts = jnp.zeros((n_experts,), jnp.int32).at[e_flat].add(1)
    tiles_per = (counts + MOE_TM - 1) // MOE_TM
    tile_end = jnp.cumsum(tiles_per)
    pad_start = (tile_end - tiles_per) * MOE_TM
    count_start = jnp.cumsum(counts) - counts
    dest = pad_start[sorted_e] + jnp.arange(na, dtype=jnp.int32) - count_start[sorted_e]
    row_token = jnp.zeros((nt * MOE_TM,), jnp.int32).at[dest].set(order // TOP_K)
    row_w = jnp.zeros((nt * MOE_TM,), F32).at[dest].set(wts.reshape(na)[order])
    pos = jnp.zeros((na,), jnp.int32).at[order].set(dest)
    tile_ids = jnp.arange(nt, dtype=jnp.int32)
    tile_expert = jnp.minimum(jnp.searchsorted(tile_end, tile_ids, side='right'),
                              n_experts - 1).astype(jnp.int32)
    tile_valid = (tile_ids < tile_end[-1]).astype(jnp.int32)

    xs = jnp.take(h2, row_token, axis=0)
    ys = _moe_experts(xs, row_w[:, None], tile_expert, tile_valid, w_gate, w_up, w_down, layer)
    return ys, pos


def _rms_norm(x, g):
    return x * lax.rsqrt(jnp.mean(x * x, axis=-1, keepdims=True) + RMS_EPS) * g


def _pool_diffs(u, u_past, start_pos):
    b, t, pw = u.shape
    grp = pw // len(POOL_WINDOWS)
    ext = jnp.concatenate([u_past, u], axis=1)
    pos = start_pos + jnp.arange(t)
    diffs = []
    for gi, win in enumerate(POOL_WINDOWS):
        e = ext[:, :, gi * grp:(gi + 1) * grp]
        win_sum = e[:, POOL_BUF:POOL_BUF + t]
        for i in range(1, win):
            win_sum = win_sum + e[:, POOL_BUF - i:POOL_BUF - i + t]
        cnt = jnp.minimum(win, pos + 1).astype(F32)[None, :, None]
        diffs.append(win_sum / cnt - e[:, POOL_BUF:])
    return jnp.concatenate(diffs, axis=-1).reshape(b * t, pw), ext[:, t:]


def kernel(x_prompt, x_sample, state_shift, state_pool, state_wkv, c_prompt, c_sample, w_ada, b_ada, g_mix, g_ffn, g_final, w_in, w_vres_down, mu, mu_vres, w_pool, pool_scale, w_decay0, w_decay2, w_a0, w_a2, w_g2, w_v0, w_v2, k_k, k_a, r_k, ln_x_w, ln_x_b, w_out, w_route_group, b_route_group, w_route_expert, b_route_expert, w_exp_gate, w_exp_up, w_exp_down):
    bp, tp, d = x_prompt.shape
    bs, ts, _ = x_sample.shape
    depth = w_in.shape[0]
    n_heads = state_wkv.shape[2]
    rw = n_heads * HEAD
    pw = state_pool.shape[-1]
    n_mod = w_ada.shape[1] // d
    n_p, n_s = bp * tp, bs * ts
    n_tok = n_p + n_s
    c_main = pw + 3 * rw
    n_tail = w_in.shape[-1] - c_main
    v_lora = w_vres_down.shape[-1]
    w_lora, a_lora = w_decay2.shape[1], w_a2.shape[1]

    tm_tok = n_tok // 8
    tm_p = tp // 2
    tm_norm = math.gcd(256, tp, n_s)
    tm_prep = tm_norm
    tm_mix = math.gcd(512, tp, n_s)
    npairs = max(q for q in range(1, WKV_PAIRS + 1)
                 if (n_heads // 2) % q == 0 and pw % (q * PAIR) == 0)

    def stream(a_p, a_s):
        return jnp.concatenate([a_p.reshape(n_p, -1), a_s.reshape(n_s, -1)], axis=0)

    c_all = jnp.concatenate([c_prompt, c_sample], axis=0)
    rows_c = -(-c_all.shape[0] // 16) * 16
    c_act = jnp.pad(jax.nn.silu(c_all), ((0, rows_c - c_all.shape[0]), (0, 0))).astype(BF16)
    mod = _mm(c_act, w_ada, tm=rows_c, tn=512)[:bp + bs].reshape(bp + bs, n_mod, d)

    zeros_l = lambda *shape: jnp.zeros((1,) + shape, F32)
    w_tail = jnp.concatenate(
        [w_in[:, :, c_main:], jnp.concatenate([zeros_l(d, v_lora), w_vres_down], axis=0)], axis=-1)
    mu_main = jnp.concatenate([jnp.zeros((depth, pw), F32), mu[:, :3 * rw]], axis=-1)[:, None]
    mu_tail = jnp.concatenate(
        [mu[:, 3 * rw:], jnp.concatenate([zeros_l(v_lora), mu_vres], axis=0)], axis=-1)[:, None]
    route_pad = 128 - (w_route_group.shape[-1] + w_route_expert.shape[-1])
    w_route = jnp.concatenate(
        [w_route_group, w_route_expert, jnp.zeros((depth, d, route_pad), F32)], axis=-1)
    pool_scale3 = pool_scale[:, None]

    lora_weights = (w_decay2, w_a2, w_g2, w_v2)
    lora_vecs = (w_decay0[:, None], w_a0[:, None], k_k[:, None], k_a[:, None], w_v0[:, None])
    assert rw % pw == 0 and pw % (npairs * PAIR) == 0

    def mixer_inputs(z, zt, v_first, l, tm):
        acts = (jnp.tanh(zt[:, :w_lora]).astype(BF16),
                zt[:, w_lora:w_lora + a_lora].astype(BF16),
                jax.nn.sigmoid(zt[:, w_lora + a_lora:n_tail]).astype(BF16),
                zt[:, n_tail:].astype(BF16))
        return _prep(z, acts, v_first, lora_weights, lora_vecs, l, tm=tm, wc=pw,
                     k_off=1 + rw // pw, v_off=1 + 2 * (rw // pw))

    x = stream(x_prompt, x_sample)
    vf_p = vf_s = None
    shift_p, pool_p, wkv_p, shift_s, pool_s, wkv_s = [], [], [], [], [], []
    zero_state = jnp.zeros((bp, n_heads, HEAD, HEAD), F32)
    for l in range(depth):
        m = mod + b_ada[l]
        m_seq = m[:bp].transpose(1, 0, 2)[:, :, None]
        m_tok = jnp.repeat(m[bp:], ts, axis=0).transpose(1, 0, 2)
        x_last = jnp.concatenate([x[:n_p].reshape(bp, tp, d)[:, -1], x[n_p:].reshape(bs, ts, d)[:, -1]])
        h_last = _rms_norm(x_last, g_mix[l]) * (1 + m[:, 1]) + m[:, 0]
        shift_p.append(h_last[:bp])
        shift_s.append(h_last[bp:])
        rows256 = _RowMod(m_seq, m_tok, tp, tm_norm)
        rows512 = _RowMod(m_seq, m_tok, tp, tm_mix)
        h_bf = _norm_mod(x, g_mix[l], rows256, 1, 0)
        hs_bf = jnp.concatenate([state_shift[l][:, None].astype(BF16), h_bf[n_p:].reshape(bs, ts, d)],
                                axis=1).reshape(bs * (ts + 1), d)

        z_p = _mm_lerp(h_bf, w_in, mu_main, l, tm=tm_p, tn=512, m=n_p, n=c_main,
                       tiles_per_seq=tp // tm_p)
        zt_p = _mm_lerp(h_bf, w_tail, mu_tail, l, tm=tm_p, tn=n_tail + v_lora, m=n_p,
                        n=n_tail + v_lora, tiles_per_seq=tp // tm_p)
        drop_first = lambda a_: a_.reshape(bs, ts + 1, -1)[:, 1:].reshape(n_s, -1)
        z_s = drop_first(_mm_lerp(hs_bf, w_in, mu_main, l, tm=hs_bf.shape[0], tn=512,
                                  m=hs_bf.shape[0], n=c_main, tiles_per_seq=None))
        zt_s = drop_first(_mm_lerp(hs_bf, w_tail, mu_tail, l, tm=hs_bf.shape[0],
                                   tn=n_tail + v_lora, m=hs_bf.shape[0], n=n_tail + v_lora,
                                   tiles_per_seq=None))

        d_p, np_p = _pool_diffs(z_p[:, :pw].reshape(bp, tp, pw), jnp.zeros((bp, POOL_BUF, pw), F32), 0)
        d_s, np_s = _pool_diffs(z_s[:, :pw].reshape(bs, ts, pw), state_pool[l], PAST_LEN)
        pool_p.append(np_p)
        pool_s.append(np_s)
        pool_out = _pool_mm(jnp.concatenate([d_p, d_s], axis=0).astype(BF16), w_pool, pool_scale3,
                            l, tm=tm_tok)

        in_p = mixer_inputs(z_p, zt_p, vf_p, l, tm_prep)
        in_s = mixer_inputs(z_s, zt_s, vf_s, l, tm_prep)
        if l == 0:
            vf_p, vf_s = in_p[2], in_s[2]
        seq_p = lambda a_: a_.reshape(bp, tp, -1)
        seq_s = lambda a_: a_.reshape(bs, ts, rw)
        o_p, s_p = _wkv_chunked(seq_p(z_p), *[seq_p(a_) for a_ in in_p], r_k[l], ln_x_w[l], ln_x_b[l],
                                zero_state, npairs=npairs, r_off=pw)
        ld_s, k_s, v_s, kk_s, kka_s, g_s = in_s
        r_s = z_s[:, pw:pw + rw]
        o_s, s_s = _wkv(seq_s(r_s), seq_s(jnp.exp(ld_s)), seq_s(k_s), seq_s(v_s), seq_s(kk_s),
                        seq_s(kka_s), state_wkv[l], tc=ts)
        wkv_p.append(s_p)
        wkv_s.append(s_s)
        heads = lambda a_: a_.reshape(n_s, n_heads, HEAD)
        o_s = heads(o_s)
        mean = jnp.mean(o_s, axis=-1, keepdims=True)
        var = jnp.mean(jnp.square(o_s - mean), axis=-1, keepdims=True)
        o_s = ((o_s - mean) * lax.rsqrt(var + GN_EPS)).reshape(n_s, rw) * ln_x_w[l] + ln_x_b[l]
        bonus = jnp.sum(heads(r_s) * heads(k_s) * r_k[l], axis=-1, keepdims=True) * heads(v_s)
        o_s = ((o_s + bonus.reshape(n_s, rw)) * g_s).astype(BF16)
        o = jnp.concatenate([o_p.reshape(n_p, rw), o_s], axis=0)
        x = _mix_out(pool_out, o, w_out, x, rows512, 2, l, tn=512)

        h2 = _norm_mod(x, g_ffn[l], rows256, 4, 3)
        ys, pos = _hier_moe(h2, l, w_route, b_route_group, b_route_expert,
                            w_exp_gate, w_exp_up, w_exp_down, tm_route=tm_tok)
        x = _moe_combine(x, ys, pos, rows256, 5)

    y = _rms_norm(x, g_final)
    return (y[:n_p].reshape(bp, tp, d), y[n_p:].reshape(bs, ts, d),
            jnp.stack(shift_p), jnp.stack(pool_p), jnp.stack(wkv_p),
            jnp.stack(shift_s), jnp.stack(pool_s), jnp.stack(wkv_s))
```

```python
import functools
import math

import jax
import jax.numpy as jnp
from jax import lax
from jax.experimental import pallas as pl
from jax.experimental.pallas import tpu as pltpu

F32 = jnp.float32
BF16 = jnp.bfloat16

HEAD = 64
PAIR = 2 * HEAD
SUBLANES = 8
POOL_WINDOWS = (2, 4, 8, 16)
POOL_BUF = max(POOL_WINDOWS) - 1
EXP_PER_GROUP = 8
TOP_K = 2
RMS_EPS = 1e-6
GN_EPS = 64e-5
MOE_TM = 256
PAST_LEN = 16384
WKV_PAIRS = 8


def _dot_bf16(a_ref, b_ref):
    return jnp.dot(a_ref[...].astype(BF16), b_ref[...].astype(BF16), preferred_element_type=F32)


def _mm_kernel(a_ref, b_ref, o_ref):
    o_ref[...] = _dot_bf16(a_ref, b_ref).astype(o_ref.dtype)


def _mm(a, b, *, tm, tn, b_lead=(), n=None, m=None, out_dtype=F32):
    k = a.shape[1]
    m = a.shape[0] if m is None else m
    n = b.shape[-1] if n is None else n
    assert b.shape[-2] == k and m % tm == 0 and n % tn == 0, (a.shape, b.shape, m, tm, n, tn)
    lead = tuple(b_lead)
    return pl.pallas_call(
        _mm_kernel,
        grid=(m // tm, n // tn),
        in_specs=[pl.BlockSpec((tm, k), lambda i, j: (i, 0)),
                  pl.BlockSpec((None,) * len(lead) + (k, tn), lambda i, j: lead + (0, j))],
        out_specs=pl.BlockSpec((tm, tn), lambda i, j: (i, j)),
        out_shape=jax.ShapeDtypeStruct((m, n), out_dtype),
        compiler_params=pltpu.CompilerParams(dimension_semantics=("parallel", "arbitrary")),
        name="mm",
    )(a, b)


def _mm_lerp_kernel(a_ref, b_ref, mu_ref, o_ref, carry_ref, *, tiles_per_seq):
    i, j = pl.program_id(0), pl.program_id(1)
    acc = _dot_bf16(a_ref, b_ref)
    prev = pltpu.roll(acc, 1, axis=0)
    if tiles_per_seq is not None:
        @pl.when(i == 0)
        def _():
            carry_ref[j] = jnp.zeros(carry_ref.shape[1:], F32)

        row = lax.broadcasted_iota(jnp.int32, acc.shape, 0)
        last = jnp.where(i % tiles_per_seq == 0, 0., carry_ref[j][SUBLANES - 1:])
        prev = jnp.where(row == 0, last, prev)
        carry_ref[j] = acc[acc.shape[0] - SUBLANES:]
    o_ref[...] = acc + (prev - acc) * mu_ref[...]


def _mm_lerp(a, b, mu_cols, layer, *, tm, tn, m, n, tiles_per_seq):
    k = a.shape[1]
    assert m % tm == 0 and n % tn == 0
    return pl.pallas_call(
        functools.partial(_mm_lerp_kernel, tiles_per_seq=tiles_per_seq),
        grid=(m // tm, n // tn),
        in_specs=[pl.BlockSpec((tm, k), lambda i, j: (i, 0)),
                  pl.BlockSpec((None, k, tn), lambda i, j: (layer, 0, j)),
                  pl.BlockSpec((None, 1, tn), lambda i, j: (layer, 0, j))],
        out_specs=pl.BlockSpec((tm, tn), lambda i, j: (i, j)),
        out_shape=jax.ShapeDtypeStruct((m, n), F32),
        scratch_shapes=[pltpu.VMEM((n // tn, SUBLANES, tn), F32)],
        compiler_params=pltpu.CompilerParams(dimension_semantics=("arbitrary", "arbitrary")),
        name="proj_lerp",
    )(a, b, mu_cols)


def _pool_mm_kernel(a_ref, b_ref, s_ref, o_ref):
    o_ref[...] = (_dot_bf16(a_ref, b_ref) * s_ref[...]).astype(o_ref.dtype)


def _pool_mm(d, w_pool, scale, layer, *, tm):
    m = d.shape[0]
    g, c = w_pool.shape[1], w_pool.shape[2]
    return pl.pallas_call(
        _pool_mm_kernel,
        grid=(m // tm, g),
        in_specs=[pl.BlockSpec((tm, c), lambda i, j: (i, j)),
                  pl.BlockSpec((None, None, c, c), lambda i, j: (layer, j, 0, 0)),
                  pl.BlockSpec((None, 1, c), lambda i, j: (layer, 0, j))],
        out_specs=pl.BlockSpec((tm, c), lambda i, j: (i, j)),
        out_shape=jax.ShapeDtypeStruct((m, g * c), BF16),
        compiler_params=pltpu.CompilerParams(dimension_semantics=("parallel", "arbitrary")),
        name="pool_mm",
    )(d, w_pool, scale)


class _RowMod:
    def __init__(self, m_seq, m_tok, rows_per_seq, tm):
        self.m_seq, self.m_tok, self.tm = m_seq, m_tok, tm
        self.n_seq = m_seq.shape[1]
        assert rows_per_seq % tm == 0 and m_tok.shape[1] % tm == 0
        self.tiles_per_seq = rows_per_seq // tm
        self.seq_tiles = self.n_seq * self.tiles_per_seq
        self.n_rows = self.n_seq * rows_per_seq + m_tok.shape[1]

    def specs(self, comp, tn, row_col):
        def seq_map(*ids):
            i, j = row_col(*ids)
            return comp, jnp.minimum(i // self.tiles_per_seq, self.n_seq - 1), 0, j

        def tok_map(*ids):
            i, j = row_col(*ids)
            return comp, jnp.maximum(i - self.seq_tiles, 0), j

        return [pl.BlockSpec((None, None, 1, tn), seq_map), pl.BlockSpec((None, self.tm, tn), tok_map)]

    def pick(self, row_tile, seq_ref, tok_ref):
        return jnp.where(row_tile < self.seq_tiles, seq_ref[...], tok_ref[...])


def _mix_out_kernel(p_ref, o1_ref, o2_ref, o3_ref, w0_ref, w1_ref, w2_ref, w3_ref, x_ref,
                    g_seq_ref, g_tok_ref, y_ref, *, mod):
    acc = (_dot_bf16(p_ref, w0_ref) + _dot_bf16(o1_ref, w1_ref)
           + _dot_bf16(o2_ref, w2_ref) + _dot_bf16(o3_ref, w3_ref))
    y_ref[...] = x_ref[...] + mod.pick(pl.program_id(1), g_seq_ref, g_tok_ref) * acc


def _mix_out(pool_out, o, w_out, x, mod, gate_i, layer, *, tn):
    n, d = x.shape
    tm = mod.tm
    q = pool_out.shape[1]
    assert o.shape[1] == 3 * q and d == 4 * q and n == mod.n_rows
    w4 = w_out.reshape(w_out.shape[0], 4, q, d)
    a_spec = lambda c: pl.BlockSpec((tm, q), lambda j, i: (i, c))
    w_spec = lambda c: pl.BlockSpec((None, None, q, tn), lambda j, i: (layer, c, 0, j))
    tile = pl.BlockSpec((tm, tn), lambda j, i: (i, j))
    return pl.pallas_call(
        functools.partial(_mix_out_kernel, mod=mod),
        grid=(d // tn, n // tm),
        in_specs=[a_spec(0), a_spec(0), a_spec(1), a_spec(2),
                  w_spec(0), w_spec(1), w_spec(2), w_spec(3), tile]
        + mod.specs(gate_i, tn, lambda j, i: (i, j)),
        out_specs=tile,
        out_shape=jax.ShapeDtypeStruct((n, d), F32),
        compiler_params=pltpu.CompilerParams(dimension_semantics=("parallel", "arbitrary")),
        name="mix_out",
    )(pool_out, o, o, o, w4, w4, w4, w4, x, mod.m_seq, mod.m_tok)


def _norm_mod_kernel(x_ref, g_ref, sc_seq_ref, sc_tok_ref, sh_seq_ref, sh_tok_ref, o_ref, *, mod):
    x = x_ref[...]
    y = x * lax.rsqrt(jnp.mean(x * x, axis=-1, keepdims=True) + RMS_EPS) * g_ref[...]
    i = pl.program_id(0)
    scale = mod.pick(i, sc_seq_ref, sc_tok_ref)
    shift = mod.pick(i, sh_seq_ref, sh_tok_ref)
    o_ref[...] = (y * (1 + scale) + shift).astype(o_ref.dtype)


def _norm_mod(x, g, mod, scale_i, shift_i):
    n, d = x.shape
    tm = mod.tm
    assert n == mod.n_rows
    row_col = lambda i: (i, 0)
    return pl.pallas_call(
        functools.partial(_norm_mod_kernel, mod=mod),
        grid=(n // tm,),
        in_specs=[pl.BlockSpec((tm, d), lambda i: (i, 0)), pl.BlockSpec((1, d), lambda i: (0, 0))]
        + mod.specs(scale_i, d, row_col) + mod.specs(shift_i, d, row_col),
        out_specs=pl.BlockSpec((tm, d), lambda i: (i, 0)),
        out_shape=jax.ShapeDtypeStruct((n, d), BF16),
        compiler_params=pltpu.CompilerParams(dimension_semantics=("parallel",)),
        name="norm_mod",
    )(x, g.reshape(1, d), mod.m_seq, mod.m_tok, mod.m_seq, mod.m_tok)


def _moe_combine_kernel(pos_ref, x_ref, w_ref, g_seq_ref, g_tok_ref, ys_ref, o_ref, buf_ref, sem, *, mod):
    tm = mod.tm
    i = pl.program_id(0)
    base = i * (tm * TOP_K)

    def row_copy(t, k):
        return pltpu.make_async_copy(ys_ref.at[pl.ds(pos_ref[base + t * TOP_K + k], 1)],
                                     buf_ref.at[k, pl.ds(t, 1)], sem)

    def issue(t, carry):
        for k in range(TOP_K):
            row_copy(t, k).start()
        return carry

    def drain(t, carry):
        for k in range(TOP_K):
            row_copy(t, k).wait()
        return carry

    lax.fori_loop(0, tm, issue, 0)
    lax.fori_loop(0, tm, drain, 0)
    moe = w_ref[:, 0:1] * buf_ref[0]
    for k in range(1, TOP_K):
        moe = moe + w_ref[:, k:k + 1] * buf_ref[k]
    o_ref[...] = x_ref[...] + mod.pick(i, g_seq_ref, g_tok_ref) * moe


def _moe_combine(x, ys, pos, wts, mod, gate_i):
    n, d = x.shape
    tm = mod.tm
    assert n == mod.n_rows and pos.shape == (n * TOP_K,)
    tile = pl.BlockSpec((tm, d), lambda i, pos_ref: (i, 0))
    return pl.pallas_call(
        functools.partial(_moe_combine_kernel, mod=mod),
        grid_spec=pltpu.PrefetchScalarGridSpec(
            num_scalar_prefetch=1, grid=(n // tm,),
            in_specs=[tile, pl.BlockSpec((tm, TOP_K), lambda i, pos_ref: (i, 0))]
            + mod.specs(gate_i, d, lambda i, pos_ref: (i, 0))
            + [pl.BlockSpec(memory_space=pl.ANY)],
            out_specs=tile,
            scratch_shapes=[pltpu.VMEM((TOP_K, tm, d), F32), pltpu.SemaphoreType.DMA(())]),
        out_shape=jax.ShapeDtypeStruct((n, d), F32),
        compiler_params=pltpu.CompilerParams(dimension_semantics=("arbitrary",)),
        name="moe_combine",
    )(pos, x, wts, mod.m_seq, mod.m_tok, ys)


def _seg_sum(x, ones_blockdiag):
    hi = x.astype(BF16)
    lo = (x - hi.astype(F32)).astype(BF16)
    return (jnp.dot(hi, ones_blockdiag, preferred_element_type=F32)
            + jnp.dot(lo, ones_blockdiag, preferred_element_type=F32))


def _ones_blockdiag():
    same_head = ((lax.broadcasted_iota(jnp.int32, (PAIR, PAIR), 0) >= HEAD)
                 == (lax.broadcasted_iota(jnp.int32, (PAIR, PAIR), 1) >= HEAD))
    return same_head.astype(BF16)


def _prep_kernel(*refs, first_layer):
    if first_layer:
        (k_ref, v_ref, twd_ref, ad_ref, sgd_ref, wd2_ref, wa2_ref, wg2_ref, d0_ref, a0_ref, kk_w_ref,
         ka_ref, ld_o, kmod_o, v_o, kk_o, kka_o, g_o) = refs
    else:
        (k_ref, v_ref, twd_ref, ad_ref, sgd_ref, wd2_ref, wa2_ref, wg2_ref, d0_ref, a0_ref, kk_w_ref,
         ka_ref, vf_ref, vd_ref, wv2_ref, v0_ref, ld_o, kmod_o, v_o, kk_o, kka_o, g_o) = refs
    k = k_ref[...]
    y = -(d0_ref[...] + _dot_bf16(twd_ref, wd2_ref))
    softplus = jnp.maximum(y, 0.) + jnp.log(1. + jnp.exp(-jnp.abs(y)))
    ld_o[...] = -jnp.exp(-softplus - 0.5)
    a = jax.nn.sigmoid(a0_ref[...] + _dot_bf16(ad_ref, wa2_ref))
    g_o[...] = _dot_bf16(sgd_ref, wg2_ref)
    v = v_ref[...]
    if not first_layer:
        v = v + (vf_ref[...] - v) * jax.nn.sigmoid(v0_ref[...] + _dot_bf16(vd_ref, wv2_ref))
    v_o[...] = v
    kk = k * kk_w_ref[...]
    tm, wc = kk.shape
    nblk = wc // PAIR
    sq = jnp.concatenate([(kk * kk)[:, q * PAIR:(q + 1) * PAIR] for q in range(nblk)], axis=0)
    ss = _seg_sum(sq, _ones_blockdiag())
    ss = jnp.concatenate([ss[q * tm:(q + 1) * tm] for q in range(nblk)], axis=1)
    kk = kk / jnp.maximum(jnp.sqrt(ss), 1e-12)
    kk_o[...] = kk
    kka_o[...] = kk * a
    kmod_o[...] = k * (1 + (a - 1) * ka_ref[...])


def _prep(z, acts, v_first, weights, vecs, layer, *, tm, wc, k_off, v_off):
    n = z.shape[0]
    twd, ad, sgd, vd = acts
    w_d2, w_a2, w_g2, w_v2 = weights
    d0, a0, kk_w, ka, v0 = vecs
    rw = w_d2.shape[-1]
    first_layer = v_first is None
    col = lambda off: pl.BlockSpec((tm, wc), lambda i, j: (i, j + off))
    act = lambda a_: pl.BlockSpec((tm, a_.shape[1]), lambda i, j: (i, 0))
    lw = lambda w_, l_: pl.BlockSpec((None, w_.shape[1], wc), lambda i, j: (l_, 0, j))
    vec = lambda l_: pl.BlockSpec((None, 1, wc), lambda i, j: (l_, 0, j))
    args = [z, z, twd, ad, sgd, w_d2, w_a2, w_g2, d0, a0, kk_w, ka]
    specs = [col(k_off), col(v_off), act(twd), act(ad), act(sgd), lw(w_d2, layer), lw(w_a2, layer),
             lw(w_g2, layer), vec(layer), vec(layer), vec(layer), vec(layer)]
    if not first_layer:
        args += [v_first, vd, w_v2, v0]
        specs += [col(0), act(vd), lw(w_v2, layer - 1), vec(layer - 1)]
    out = jax.ShapeDtypeStruct((n, rw), F32)
    return pl.pallas_call(
        functools.partial(_prep_kernel, first_layer=first_layer),
        grid=(n // tm, rw // wc),
        in_specs=specs,
        out_specs=[col(0)] * 6,
        out_shape=[out] * 6,
        compiler_params=pltpu.CompilerParams(dimension_semantics=("parallel", "arbitrary")),
        name="rwkv_prep",
    )(*args)


def _wkv_kernel(r_ref, d_ref, k_ref, v_ref, kk_ref, kka_ref, s0_ref, o_ref, s_ref, *, tc, npairs):
    @pl.when(pl.program_id(1) == 0)
    def _():
        s_ref[...] = s0_ref[...]

    lane = lax.broadcasted_iota(jnp.int32, (HEAD, PAIR), 1)
    row = lax.broadcasted_iota(jnp.int32, (HEAD, PAIR), 0)
    diag = ((lane & (HEAD - 1)) == row).astype(F32)
    jr = lax.broadcasted_iota(jnp.int32, (PAIR, PAIR), 0) >= HEAD
    jc = lax.broadcasted_iota(jnp.int32, (PAIR, PAIR), 1) >= HEAD
    ones_blockdiag = (jr == jc).astype(BF16)

    rows = min(tc, SUBLANES)
    row_id = lax.broadcasted_iota(jnp.int32, (rows, PAIR), 0)

    def row_group(gi, carry):
        base = pl.multiple_of(gi * rows, rows)

        def rowvec(ref, p, i):
            return ref[0, pl.ds(base, rows), p * PAIR:(p + 1) * PAIR][i:i + 1]

        o_tiles = [jnp.zeros((rows, PAIR), F32)] * npairs
        for i in range(rows):
            sk = jnp.concatenate([s_ref[0, p] * rowvec(kk_ref, p, i) for p in range(npairs)], axis=0)
            vd = jnp.concatenate([diag * rowvec(v_ref, p, i) for p in range(npairs)], axis=0)
            y_kk = jnp.dot(sk.astype(BF16), ones_blockdiag, preferred_element_type=F32)
            y_v = _seg_sum(vd, ones_blockdiag)
            qs = []
            for p in range(npairs):
                s_kk = y_kk[p * HEAD:(p + 1) * HEAD]
                v_col = y_v[p * HEAD:(p + 1) * HEAD]
                s = (s_ref[0, p] * rowvec(d_ref, p, i) - s_kk * rowvec(kka_ref, p, i)
                     + v_col * rowvec(k_ref, p, i))
                s_ref[0, p] = s
                qs.append(s * rowvec(r_ref, p, i))
            z = jnp.dot(jnp.concatenate(qs, axis=0).astype(BF16), ones_blockdiag,
                        preferred_element_type=F32)
            for p in range(npairs):
                o_col = z[p * HEAD:(p + 1) * HEAD]
                o_row = jnp.sum(o_col * diag, axis=0, keepdims=True)
                o_tiles[p] = jnp.where(row_id == i, o_row, o_tiles[p])
        for p in range(npairs):
            o_ref[0, pl.ds(base, rows), p * PAIR:(p + 1) * PAIR] = o_tiles[p]
        return carry

    lax.fori_loop(0, tc // rows, row_group, 0)


def _pair_state(s):
    b, h = s.shape[:2]
    return s.reshape(b, h // 2, 2, HEAD, HEAD).transpose(0, 1, 3, 2, 4).reshape(b, h // 2, HEAD, PAIR)


def _unpair_state(s):
    b, hp = s.shape[:2]
    return s.reshape(b, hp, HEAD, 2, HEAD).transpose(0, 1, 3, 2, 4).reshape(b, 2 * hp, HEAD, HEAD)


def _wkv(r, decay, k, v, kk, kka, s0, *, tc):
    b, t, w = r.shape
    npairs = w // PAIR
    assert t % tc == 0
    seq = pl.BlockSpec((1, tc, w), lambda i, c: (i, c, 0))
    st = pl.BlockSpec((1, npairs, HEAD, PAIR), lambda i, c: (i, 0, 0, 0))
    o, s = pl.pallas_call(
        functools.partial(_wkv_kernel, tc=tc, npairs=npairs),
        grid=(b, t // tc),
        in_specs=[seq] * 6 + [st],
        out_specs=[seq, st],
        out_shape=[jax.ShapeDtypeStruct((b, t, w), F32),
                   jax.ShapeDtypeStruct((b, npairs, HEAD, PAIR), F32)],
        compiler_params=pltpu.CompilerParams(dimension_semantics=("parallel", "arbitrary")),
        name="wkv_steps",
    )(r, decay, k, v, kk, kka, _pair_state(s0))
    return o, _unpair_state(s)


def _bdot(x, y):
    return jnp.dot(x.astype(BF16), y.astype(BF16), preferred_element_type=F32)


def _bdot_nt(x, y):
    return lax.dot_general(x.astype(BF16), y.astype(BF16), (((1,), (1,)), ((), ())),
                           preferred_element_type=F32)


def _split3(x):
    h1 = x.astype(BF16)
    r1 = x - h1.astype(F32)
    h2 = r1.astype(BF16)
    return h1, h2, (r1 - h2.astype(F32)).astype(BF16)


def _wkv_chunk_kernel(r_ref, ld_ref, k_ref, v_ref, kk_ref, kka_ref, g_ref, rk_ref, lnw_ref, lnb_ref,
                      s0_ref, o_ref, s_ref, *, npairs):
    c = HEAD

    @pl.when(pl.program_id(2) == 0)
    def _():
        s_ref[...] = s0_ref[...]

    row = lax.broadcasted_iota(jnp.int32, (c, PAIR), 0)
    col = lax.broadcasted_iota(jnp.int32, (c, PAIR), 1)
    left = col < c
    strict = (col & (c - 1)) < row
    incl = (col & (c - 1)) <= row
    tri = (lax.broadcasted_iota(jnp.int32, (c, c), 1)
           <= lax.broadcasted_iota(jnp.int32, (c, c), 0)).astype(BF16)
    same_head = ((lax.broadcasted_iota(jnp.int32, (PAIR, PAIR), 0) >= HEAD)
                 == (lax.broadcasted_iota(jnp.int32, (PAIR, PAIR), 1) >= HEAD))
    ones_blockdiag = same_head.astype(BF16)

    def halves(x):
        return jnp.concatenate([jnp.where(left, x, 0.), jnp.where(left, 0., x)], axis=1)

    def pairs(x):
        return jnp.stack([x[:, p * PAIR:(p + 1) * PAIR] for p in range(npairs)])

    def bmm(x, y):
        return lax.dot_general(x.astype(BF16), y.astype(BF16), (((2,), (1,)), ((0,), (0,))),
                               preferred_element_type=F32)

    def bmm_nt(x, y):
        return lax.dot_general(x.astype(BF16), y.astype(BF16), (((2,), (2,)), ((0,), (0,))),
                               preferred_element_type=F32)

    def seg_sum(x):
        return _seg_sum(x.reshape(npairs * c, PAIR), ones_blockdiag).reshape(npairs, c, PAIR)

    ld = ld_ref[0]
    r, k, v = r_ref[0], k_ref[0], v_ref[0]
    b = sum(jnp.dot(tri, part, preferred_element_type=F32) for part in _split3(ld))
    eb = jnp.exp(b)
    e_inv = jnp.exp(-b)
    a_t = pairs(-kk_ref[0] * jnp.exp(b - ld))
    r_t = pairs(r * eb)
    k_t = pairs(k * e_inv)
    b_t = pairs(kka_ref[0] * e_inv)
    eb_last = pairs(eb[c - 1:c])
    v_p = pairs(v)
    s = s_ref[0]

    gram = bmm_nt(jnp.concatenate([a_t, r_t], axis=1),
                  jnp.concatenate([halves(b_t), halves(k_t)], axis=1))
    l_ab = jnp.where(strict, gram[:, :c, :PAIR], 0.)
    l_ak = jnp.where(strict, gram[:, :c, PAIR:], 0.)
    m_rb = jnp.where(incl, gram[:, c:, :PAIR], 0.)
    m_rk = jnp.where(incl, gram[:, c:, PAIR:], 0.)

    v_h = halves(v_p)
    u = bmm_nt(a_t, s) + bmm(l_ak, v_h)
    l_pow = l_ab
    n = 1
    while True:
        u = u + bmm(l_pow, halves(u))
        n *= 2
        if n >= c:
            break
        l_pow = bmm(l_pow, halves(l_pow))
    o = bmm_nt(r_t, s) + bmm(jnp.concatenate([m_rk, m_rb], axis=2),
                             jnp.concatenate([v_h, halves(u)], axis=1))
    vu = jnp.concatenate([v_p, u], axis=1)
    vu_t = jnp.stack([vu[p].T for p in range(npairs)])
    s_new = bmm(vu_t, jnp.concatenate([k_t, b_t], axis=1))
    s_ref[0] = (s + jnp.where(same_head, s_new, 0.)) * eb_last

    mean = seg_sum(o) * (1.0 / HEAD)
    dev = o - mean
    var = seg_sum(dev * dev) * (1.0 / HEAD)
    bonus = seg_sum(pairs(r * k * rk_ref[...])) * v_p
    o = dev * lax.rsqrt(var + GN_EPS) * pairs(lnw_ref[...]) + pairs(lnb_ref[...]) + bonus
    o = o * pairs(g_ref[0])
    for p in range(npairs):
        o_ref[0, :, p * PAIR:(p + 1) * PAIR] = o[p].astype(o_ref.dtype)


def _blockdiag_state(s):
    b, h = s.shape[:2]
    s = s.reshape(b, h // 2, 2, HEAD, HEAD)
    z = jnp.zeros_like(s[:, :, 0])
    return jnp.concatenate([jnp.concatenate([s[:, :, 0], z], axis=-1),
                            jnp.concatenate([z, s[:, :, 1]], axis=-1)], axis=-2)


def _unblockdiag_state(s):
    b, hp = s.shape[:2]
    return jnp.stack([s[:, :, :HEAD, :HEAD], s[:, :, HEAD:, HEAD:]], axis=2).reshape(
        b, 2 * hp, HEAD, HEAD)


def _wkv_chunked(r, log_decay, k, v, kk, kka, g, r_k, ln_w, ln_b, s0, *, npairs, r_off=0):
    b, t, w = log_decay.shape
    wb = npairs * PAIR
    assert t % HEAD == 0 and (w // PAIR) % npairs == 0 and r_off % wb == 0
    seq = pl.BlockSpec((1, HEAD, wb), lambda i, j, c: (i, c, j))
    r_seq = pl.BlockSpec((1, HEAD, wb), lambda i, j, c: (i, c, j + r_off // wb))
    vec = pl.BlockSpec((1, wb), lambda i, j, c: (0, j))
    st = pl.BlockSpec((1, npairs, PAIR, PAIR), lambda i, j, c: (i, j, 0, 0))
    o, s = pl.pallas_call(
        functools.partial(_wkv_chunk_kernel, npairs=npairs),
        grid=(b, w // wb, t // HEAD),
        in_specs=[r_seq] + [seq] * 6 + [vec] * 3 + [st],
        out_specs=[seq, st],
        out_shape=[jax.ShapeDtypeStruct((b, t, w), BF16),
                   jax.ShapeDtypeStruct((b, w // PAIR, PAIR, PAIR), F32)],
        compiler_params=pltpu.CompilerParams(
            dimension_semantics=("parallel", "parallel", "arbitrary")),
        name="wkv_chunked",
    )(r, log_decay, k, v, kk, kka, g, r_k.reshape(1, w), ln_w.reshape(1, w), ln_b.reshape(1, w),
      _blockdiag_state(s0))
    return o, _unblockdiag_state(s)


def _moe_up_kernel(te_ref, tv_ref, x_ref, wg_ref, wu_ref, h_ref):
    i = pl.program_id(0)

    @pl.when(tv_ref[i] > 0)
    def _():
        x = x_ref[...]
        g = jnp.dot(x, wg_ref[...].astype(BF16), preferred_element_type=F32)
        u = jnp.dot(x, wu_ref[...].astype(BF16), preferred_element_type=F32)
        h_ref[...] = (g * jax.nn.sigmoid(g) * u).astype(h_ref.dtype)

    @pl.when(tv_ref[i] == 0)
    def _():
        h_ref[...] = jnp.zeros_like(h_ref)


def _moe_down_kernel(te_ref, tv_ref, h_ref, wd_ref, y_ref):
    i = pl.program_id(0)

    @pl.when(tv_ref[i] > 0)
    def _():
        y_ref[...] = jnp.dot(h_ref[...], wd_ref[...].astype(BF16), preferred_element_type=F32)

    @pl.when(tv_ref[i] == 0)
    def _():
        y_ref[...] = jnp.zeros_like(y_ref)


def _moe_experts(xs, tile_expert, tile_valid, w_gate, w_up, w_down, layer):
    p, d = xs.shape
    de = w_gate.shape[-1]
    nt = p // MOE_TM
    hid = pl.pallas_call(
        _moe_up_kernel,
        grid_spec=pltpu.PrefetchScalarGridSpec(
            num_scalar_prefetch=2, grid=(nt,),
            in_specs=[pl.BlockSpec((MOE_TM, d), lambda i, te, tv: (i, 0)),
                      pl.BlockSpec((None, None, d, de), lambda i, te, tv: (layer, te[i], 0, 0)),
                      pl.BlockSpec((None, None, d, de), lambda i, te, tv: (layer, te[i], 0, 0))],
            out_specs=pl.BlockSpec((MOE_TM, de), lambda i, te, tv: (i, 0))),
        out_shape=jax.ShapeDtypeStruct((p, de), BF16),
        compiler_params=pltpu.CompilerParams(dimension_semantics=("arbitrary",)),
        name="moe_up",
    )(tile_expert, tile_valid, xs, w_gate, w_up)
    return pl.pallas_call(
        _moe_down_kernel,
        grid_spec=pltpu.PrefetchScalarGridSpec(
            num_scalar_prefetch=2, grid=(nt,),
            in_specs=[pl.BlockSpec((MOE_TM, de), lambda i, te, tv: (i, 0)),
                      pl.BlockSpec((None, None, de, d), lambda i, te, tv: (layer, te[i], 0, 0))],
            out_specs=pl.BlockSpec((MOE_TM, d), lambda i, te, tv: (i, 0))),
        out_shape=jax.ShapeDtypeStruct((p, d), F32),
        compiler_params=pltpu.CompilerParams(dimension_semantics=("arbitrary",)),
        name="moe_down",
    )(tile_expert, tile_valid, hid, w_down)


def _hier_moe(h2, layer, w_route, b_route_group, b_route_expert, w_gate, w_up, w_down, *, tm_route):
    n, d = h2.shape
    n_groups = b_route_group.shape[-1]
    n_experts = b_route_expert.shape[-1]
    logits = _mm(h2, w_route, tm=tm_route, tn=w_route.shape[-1], b_lead=(layer,))
    p_group = jax.nn.softmax(logits[:, :n_groups] + b_route_group[layer], axis=-1)
    g_idx = jnp.argmax(p_group, axis=-1, keepdims=True)
    p_top = jnp.max(p_group, axis=-1, keepdims=True)
    le = (logits[:, n_groups:n_groups + n_experts] + b_route_expert[layer])
    le = le.reshape(n, n_groups, EXP_PER_GROUP)
    le = jnp.take_along_axis(le, g_idx[:, :, None], axis=1)[:, 0]
    v_top, e_idx, rest = [], [], le
    for _ in range(TOP_K):
        best = jnp.argmax(rest, axis=-1, keepdims=True)
        v_top.append(jnp.max(rest, axis=-1, keepdims=True))
        e_idx.append(best)
        rest = jnp.where(jnp.arange(EXP_PER_GROUP) == best, -jnp.inf, rest)
    v_top, e_idx = jnp.concatenate(v_top, axis=-1), jnp.concatenate(e_idx, axis=-1)
    wts = p_top * jax.nn.softmax(v_top, axis=-1)
    gidx = (g_idx * EXP_PER_GROUP + e_idx).astype(jnp.int32)

    na = n * TOP_K
    nt = na // MOE_TM + n_experts
    e_flat = gidx.reshape(na)
    order = jnp.argsort(e_flat, stable=True).astype(jnp.int32)
    sorted_e = e_flat[order]
    counts = jnp.zeros((n_experts,), jnp.int32).at[e_flat].add(1)
    tiles_per = (counts + MOE_TM - 1) // MOE_TM
    tile_end = jnp.cumsum(tiles_per)
    pad_start = (tile_end - tiles_per) * MOE_TM
    count_start = jnp.cumsum(counts) - counts
    dest = pad_start[sorted_e] + jnp.arange(na, dtype=jnp.int32) - count_start[sorted_e]
    row_token = jnp.zeros((nt * MOE_TM,), jnp.int32).at[dest].set(order // TOP_K)
    pos = jnp.zeros((na,), jnp.int32).at[order].set(dest)
    tile_ids = jnp.arange(nt, dtype=jnp.int32)
    tile_expert = jnp.minimum(jnp.searchsorted(tile_end, tile_ids, side='right'),
                              n_experts - 1).astype(jnp.int32)
    tile_valid = (tile_ids < tile_end[-1]).astype(jnp.int32)

    xs = jnp.take(h2, row_token, axis=0, mode="clip")
    ys = _moe_experts(xs, tile_expert, tile_valid, w_gate, w_up, w_down, layer)
    return ys, pos, wts


def _rms_norm(x, g):
    return x * lax.rsqrt(jnp.mean(x * x, axis=-1, keepdims=True) + RMS_EPS) * g


def _pool_diffs(u, u_past, start_pos):
    b, t, pw = u.shape
    grp = pw // len(POOL_WINDOWS)
    ext = jnp.concatenate([u_past, u], axis=1)
    pos = start_pos + jnp.arange(t)
    diffs = []
    for gi, win in enumerate(POOL_WINDOWS):
        e = ext[:, :, gi * grp:(gi + 1) * grp]
        win_sum = e[:, POOL_BUF:POOL_BUF + t]
        for i in range(1, win):
            win_sum = win_sum + e[:, POOL_BUF - i:POOL_BUF - i + t]
        cnt = jnp.minimum(win, pos + 1).astype(F32)[None, :, None]
        diffs.append(win_sum / cnt - e[:, POOL_BUF:])
    return jnp.concatenate(diffs, axis=-1).reshape(b * t, pw), ext[:, t:]


def kernel(x_prompt, x_sample, state_shift, state_pool, state_wkv, c_prompt, c_sample, w_ada, b_ada, g_mix, g_ffn, g_final, w_in, w_vres_down, mu, mu_vres, w_pool, pool_scale, w_decay0, w_decay2, w_a0, w_a2, w_g2, w_v0, w_v2, k_k, k_a, r_k, ln_x_w, ln_x_b, w_out, w_route_group, b_route_group, w_route_expert, b_route_expert, w_exp_gate, w_exp_up, w_exp_down):
    bp, tp, d = x_prompt.shape
    bs, ts, _ = x_sample.shape
    depth = w_in.shape[0]
    n_heads = state_wkv.shape[2]
    rw = n_heads * HEAD
    pw = state_pool.shape[-1]
    n_mod = w_ada.shape[1] // d
    n_p, n_s = bp * tp, bs * ts
    n_tok = n_p + n_s
    c_main = pw + 3 * rw
    n_tail = w_in.shape[-1] - c_main
    v_lora = w_vres_down.shape[-1]
    w_lora, a_lora = w_decay2.shape[1], w_a2.shape[1]

    tm_tok = n_tok // 8
    tm_p = tp // 2
    tm_norm = math.gcd(256, tp, n_s)
    tm_prep = tm_norm
    tm_mix = math.gcd(512, tp, n_s)
    npairs = max(q for q in range(1, WKV_PAIRS + 1)
                 if (n_heads // 2) % q == 0 and pw % (q * PAIR) == 0)

    def stream(a_p, a_s):
        return jnp.concatenate([a_p.reshape(n_p, -1), a_s.reshape(n_s, -1)], axis=0)

    c_all = jnp.concatenate([c_prompt, c_sample], axis=0)
    rows_c = -(-c_all.shape[0] // 16) * 16
    c_act = jnp.pad(jax.nn.silu(c_all), ((0, rows_c - c_all.shape[0]), (0, 0))).astype(BF16)
    mod = _mm(c_act, w_ada, tm=rows_c, tn=512)[:bp + bs].reshape(bp + bs, n_mod, d)

    zeros_l = lambda *shape: jnp.zeros((1,) + shape, F32)
    w_tail = jnp.concatenate(
        [w_in[:, :, c_main:], jnp.concatenate([zeros_l(d, v_lora), w_vres_down], axis=0)], axis=-1)
    mu_main = jnp.concatenate([jnp.zeros((depth, pw), F32), mu[:, :3 * rw]], axis=-1)[:, None]
    mu_tail = jnp.concatenate(
        [mu[:, 3 * rw:], jnp.concatenate([zeros_l(v_lora), mu_vres], axis=0)], axis=-1)[:, None]
    route_pad = 128 - (w_route_group.shape[-1] + w_route_expert.shape[-1])
    w_route = jnp.concatenate(
        [w_route_group, w_route_expert, jnp.zeros((depth, d, route_pad), F32)], axis=-1)
    pool_scale3 = pool_scale[:, None]

    lora_weights = (w_decay2, w_a2, w_g2, w_v2)
    lora_vecs = (w_decay0[:, None], w_a0[:, None], k_k[:, None], k_a[:, None], w_v0[:, None])
    assert rw % pw == 0 and pw % (npairs * PAIR) == 0

    def mixer_inputs(z, zt, v_first, l, tm):
        acts = (jnp.tanh(zt[:, :w_lora]).astype(BF16),
                zt[:, w_lora:w_lora + a_lora].astype(BF16),
                jax.nn.sigmoid(zt[:, w_lora + a_lora:n_tail]).astype(BF16),
                zt[:, n_tail:].astype(BF16))
        return _prep(z, acts, v_first, lora_weights, lora_vecs, l, tm=tm, wc=pw,
                     k_off=1 + rw // pw, v_off=1 + 2 * (rw // pw))

    x = stream(x_prompt, x_sample)
    vf_p = vf_s = None
    shift_p, pool_p, wkv_p, shift_s, pool_s, wkv_s = [], [], [], [], [], []
    zero_state = jnp.zeros((bp, n_heads, HEAD, HEAD), F32)
    for l in range(depth):
        m = mod + b_ada[l]
        m_seq = m[:bp].transpose(1, 0, 2)[:, :, None]
        m_tok = jnp.repeat(m[bp:], ts, axis=0).transpose(1, 0, 2)
        x_last = jnp.concatenate([x[:n_p].reshape(bp, tp, d)[:, -1], x[n_p:].reshape(bs, ts, d)[:, -1]])
        h_last = _rms_norm(x_last, g_mix[l]) * (1 + m[:, 1]) + m[:, 0]
        shift_p.append(h_last[:bp])
        shift_s.append(h_last[bp:])
        rows256 = _RowMod(m_seq, m_tok, tp, tm_norm)
        rows512 = _RowMod(m_seq, m_tok, tp, tm_mix)
        h_bf = _norm_mod(x, g_mix[l], rows256, 1, 0)
        hs_bf = jnp.concatenate([state_shift[l][:, None].astype(BF16), h_bf[n_p:].reshape(bs, ts, d)],
                                axis=1).reshape(bs * (ts + 1), d)

        z_p = _mm_lerp(h_bf, w_in, mu_main, l, tm=tm_p, tn=512, m=n_p, n=c_main,
                       tiles_per_seq=tp // tm_p)
        zt_p = _mm_lerp(h_bf, w_tail, mu_tail, l, tm=tm_p, tn=n_tail + v_lora, m=n_p,
                        n=n_tail + v_lora, tiles_per_seq=tp // tm_p)
        drop_first = lambda a_: a_.reshape(bs, ts + 1, -1)[:, 1:].reshape(n_s, -1)
        z_s = drop_first(_mm_lerp(hs_bf, w_in, mu_main, l, tm=hs_bf.shape[0], tn=512,
                                  m=hs_bf.shape[0], n=c_main, tiles_per_seq=None))
        zt_s = drop_first(_mm_lerp(hs_bf, w_tail, mu_tail, l, tm=hs_bf.shape[0],
                                   tn=n_tail + v_lora, m=hs_bf.shape[0], n=n_tail + v_lora,
                                   tiles_per_seq=None))

        d_p, np_p = _pool_diffs(z_p[:, :pw].reshape(bp, tp, pw), jnp.zeros((bp, POOL_BUF, pw), F32), 0)
        d_s, np_s = _pool_diffs(z_s[:, :pw].reshape(bs, ts, pw), state_pool[l], PAST_LEN)
        pool_p.append(np_p)
        pool_s.append(np_s)
        pool_out = _pool_mm(jnp.concatenate([d_p, d_s], axis=0).astype(BF16), w_pool, pool_scale3,
                            l, tm=tm_tok)

        in_p = mixer_inputs(z_p, zt_p, vf_p, l, tm_prep)
        in_s = mixer_inputs(z_s, zt_s, vf_s, l, tm_prep)
        if l == 0:
            vf_p, vf_s = in_p[2], in_s[2]
        seq_p = lambda a_: a_.reshape(bp, tp, -1)
        seq_s = lambda a_: a_.reshape(bs, ts, rw)
        o_p, s_p = _wkv_chunked(seq_p(z_p), *[seq_p(a_) for a_ in in_p], r_k[l], ln_x_w[l], ln_x_b[l],
                                zero_state, npairs=npairs, r_off=pw)
        ld_s, k_s, v_s, kk_s, kka_s, g_s = in_s
        r_s = z_s[:, pw:pw + rw]
        o_s, s_s = _wkv(seq_s(r_s), seq_s(jnp.exp(ld_s)), seq_s(k_s), seq_s(v_s), seq_s(kk_s),
                        seq_s(kka_s), state_wkv[l], tc=ts)
        wkv_p.append(s_p)
        wkv_s.append(s_s)
        heads = lambda a_: a_.reshape(n_s, n_heads, HEAD)
        o_s = heads(o_s)
        mean = jnp.mean(o_s, axis=-1, keepdims=True)
        var = jnp.mean(jnp.square(o_s - mean), axis=-1, keepdims=True)
        o_s = ((o_s - mean) * lax.rsqrt(var + GN_EPS)).reshape(n_s, rw) * ln_x_w[l] + ln_x_b[l]
        bonus = jnp.sum(heads(r_s) * heads(k_s) * r_k[l], axis=-1, keepdims=True) * heads(v_s)
        o_s = ((o_s + bonus.reshape(n_s, rw)) * g_s).astype(BF16)
        o = jnp.concatenate([o_p.reshape(n_p, rw), o_s], axis=0)
        x = _mix_out(pool_out, o, w_out, x, rows512, 2, l, tn=512)

        h2 = _norm_mod(x, g_ffn[l], rows256, 4, 3)
        ys, pos, wts = _hier_moe(h2, l, w_route, b_route_group, b_route_expert,
                                 w_exp_gate, w_exp_up, w_exp_down, tm_route=tm_tok)
        x = _moe_combine(x, ys, pos, wts, rows256, 5)

    y = _rms_norm(x, g_final)
    return (y[:n_p].reshape(bp, tp, d), y[n_p:].reshape(bs, ts, d),
            jnp.stack(shift_p), jnp.stack(pool_p), jnp.stack(wkv_p),
            jnp.stack(shift_s), jnp.stack(pool_s), jnp.stack(wkv_s))
```

```python
import functools
import math

import jax
import jax.numpy as jnp
from jax import lax
from jax.experimental import pallas as pl
from jax.experimental.pallas import tpu as pltpu

F32 = jnp.float32
BF16 = jnp.bfloat16

HEAD = 64
PAIR = 2 * HEAD
SUBLANES = 8
POOL_WINDOWS = (2, 4, 8, 16)
POOL_BUF = max(POOL_WINDOWS) - 1
EXP_PER_GROUP = 8
TOP_K = 2
RMS_EPS = 1e-6
GN_EPS = 64e-5
MOE_TM = 256
PAST_LEN = 16384
WKV_PAIRS = 8


def _dot_bf16(a_ref, b_ref):
    return jnp.dot(a_ref[...].astype(BF16), b_ref[...].astype(BF16), preferred_element_type=F32)


def _mm_kernel(a_ref, b_ref, o_ref):
    o_ref[...] = _dot_bf16(a_ref, b_ref).astype(o_ref.dtype)


def _mm(a, b, *, tm, tn, b_lead=(), n=None, m=None, out_dtype=F32):
    k = a.shape[1]
    m = a.shape[0] if m is None else m
    n = b.shape[-1] if n is None else n
    assert b.shape[-2] == k and m % tm == 0 and n % tn == 0, (a.shape, b.shape, m, tm, n, tn)
    lead = tuple(b_lead)
    return pl.pallas_call(
        _mm_kernel,
        grid=(m // tm, n // tn),
        in_specs=[pl.BlockSpec((tm, k), lambda i, j: (i, 0)),
                  pl.BlockSpec((None,) * len(lead) + (k, tn), lambda i, j: lead + (0, j))],
        out_specs=pl.BlockSpec((tm, tn), lambda i, j: (i, j)),
        out_shape=jax.ShapeDtypeStruct((m, n), out_dtype),
        compiler_params=pltpu.CompilerParams(dimension_semantics=("parallel", "arbitrary")),
        name="mm",
    )(a, b)


def _mm_lerp_kernel(a_ref, b_ref, mu_ref, o_ref, carry_ref, *, tiles_per_seq):
    i, j = pl.program_id(0), pl.program_id(1)
    acc = _dot_bf16(a_ref, b_ref)
    prev = pltpu.roll(acc, 1, axis=0)
    if tiles_per_seq is not None:
        @pl.when(i == 0)
        def _():
            carry_ref[j] = jnp.zeros(carry_ref.shape[1:], F32)

        row = lax.broadcasted_iota(jnp.int32, acc.shape, 0)
        last = jnp.where(i % tiles_per_seq == 0, 0., carry_ref[j][SUBLANES - 1:])
        prev = jnp.where(row == 0, last, prev)
        carry_ref[j] = acc[acc.shape[0] - SUBLANES:]
    o_ref[...] = acc + (prev - acc) * mu_ref[...]


def _mm_lerp(a, b, mu_cols, layer, *, tm, tn, m, n, tiles_per_seq):
    k = a.shape[1]
    assert m % tm == 0 and n % tn == 0
    return pl.pallas_call(
        functools.partial(_mm_lerp_kernel, tiles_per_seq=tiles_per_seq),
        grid=(m // tm, n // tn),
        in_specs=[pl.BlockSpec((tm, k), lambda i, j: (i, 0)),
                  pl.BlockSpec((None, k, tn), lambda i, j: (layer, 0, j)),
                  pl.BlockSpec((None, 1, tn), lambda i, j: (layer, 0, j))],
        out_specs=pl.BlockSpec((tm, tn), lambda i, j: (i, j)),
        out_shape=jax.ShapeDtypeStruct((m, n), F32),
        scratch_shapes=[pltpu.VMEM((n // tn, SUBLANES, tn), F32)],
        compiler_params=pltpu.CompilerParams(dimension_semantics=("arbitrary", "arbitrary")),
        name="proj_lerp",
    )(a, b, mu_cols)


def _pool_mm_kernel(a_ref, b_ref, s_ref, o_ref):
    o_ref[...] = (_dot_bf16(a_ref, b_ref) * s_ref[...]).astype(o_ref.dtype)


def _pool_mm(d, w_pool, scale, layer, *, tm):
    m = d.shape[0]
    g, c = w_pool.shape[1], w_pool.shape[2]
    return pl.pallas_call(
        _pool_mm_kernel,
        grid=(m // tm, g),
        in_specs=[pl.BlockSpec((tm, c), lambda i, j: (i, j)),
                  pl.BlockSpec((None, None, c, c), lambda i, j: (layer, j, 0, 0)),
                  pl.BlockSpec((None, 1, c), lambda i, j: (layer, 0, j))],
        out_specs=pl.BlockSpec((tm, c), lambda i, j: (i, j)),
        out_shape=jax.ShapeDtypeStruct((m, g * c), BF16),
        compiler_params=pltpu.CompilerParams(dimension_semantics=("parallel", "arbitrary")),
        name="pool_mm",
    )(d, w_pool, scale)


class _RowMod:
    def __init__(self, m_seq, m_tok, rows_per_seq, tm):
        self.m_seq, self.m_tok, self.tm = m_seq, m_tok, tm
        self.n_seq = m_seq.shape[1]
        assert rows_per_seq % tm == 0 and m_tok.shape[1] % tm == 0
        self.tiles_per_seq = rows_per_seq // tm
        self.seq_tiles = self.n_seq * self.tiles_per_seq
        self.n_rows = self.n_seq * rows_per_seq + m_tok.shape[1]

    def specs(self, comp, tn, row_col):
        def seq_map(*ids):
            i, j = row_col(*ids)
            return comp, jnp.minimum(i // self.tiles_per_seq, self.n_seq - 1), 0, j

        def tok_map(*ids):
            i, j = row_col(*ids)
            return comp, jnp.maximum(i - self.seq_tiles, 0), j

        return [pl.BlockSpec((None, None, 1, tn), seq_map), pl.BlockSpec((None, self.tm, tn), tok_map)]

    def pick(self, row_tile, seq_ref, tok_ref):
        return jnp.where(row_tile < self.seq_tiles, seq_ref[...], tok_ref[...])


def _mix_out_kernel(p_ref, o1_ref, o2_ref, o3_ref, w0_ref, w1_ref, w2_ref, w3_ref, x_ref,
                    g_seq_ref, g_tok_ref, y_ref, *, mod):
    acc = (_dot_bf16(p_ref, w0_ref) + _dot_bf16(o1_ref, w1_ref)
           + _dot_bf16(o2_ref, w2_ref) + _dot_bf16(o3_ref, w3_ref))
    y_ref[...] = x_ref[...] + mod.pick(pl.program_id(1), g_seq_ref, g_tok_ref) * acc


def _mix_out(pool_out, o, w_out, x, mod, gate_i, layer, *, tn):
    n, d = x.shape
    tm = mod.tm
    q = pool_out.shape[1]
    assert o.shape[1] == 3 * q and d == 4 * q and n == mod.n_rows
    w4 = w_out.reshape(w_out.shape[0], 4, q, d)
    a_spec = lambda c: pl.BlockSpec((tm, q), lambda j, i: (i, c))
    w_spec = lambda c: pl.BlockSpec((None, None, q, tn), lambda j, i: (layer, c, 0, j))
    tile = pl.BlockSpec((tm, tn), lambda j, i: (i, j))
    return pl.pallas_call(
        functools.partial(_mix_out_kernel, mod=mod),
        grid=(d // tn, n // tm),
        in_specs=[a_spec(0), a_spec(0), a_spec(1), a_spec(2),
                  w_spec(0), w_spec(1), w_spec(2), w_spec(3), tile]
        + mod.specs(gate_i, tn, lambda j, i: (i, j)),
        out_specs=tile,
        out_shape=jax.ShapeDtypeStruct((n, d), F32),
        compiler_params=pltpu.CompilerParams(dimension_semantics=("parallel", "arbitrary")),
        name="mix_out",
    )(pool_out, o, o, o, w4, w4, w4, w4, x, mod.m_seq, mod.m_tok)


def _norm_mod_kernel(x_ref, g_ref, sc_seq_ref, sc_tok_ref, sh_seq_ref, sh_tok_ref, o_ref, *, mod):
    x = x_ref[...]
    y = x * lax.rsqrt(jnp.mean(x * x, axis=-1, keepdims=True) + RMS_EPS) * g_ref[...]
    i = pl.program_id(0)
    scale = mod.pick(i, sc_seq_ref, sc_tok_ref)
    shift = mod.pick(i, sh_seq_ref, sh_tok_ref)
    o_ref[...] = (y * (1 + scale) + shift).astype(o_ref.dtype)


def _norm_mod(x, g, mod, scale_i, shift_i):
    n, d = x.shape
    tm = mod.tm
    assert n == mod.n_rows
    row_col = lambda i: (i, 0)
    return pl.pallas_call(
        functools.partial(_norm_mod_kernel, mod=mod),
        grid=(n // tm,),
        in_specs=[pl.BlockSpec((tm, d), lambda i: (i, 0)), pl.BlockSpec((1, d), lambda i: (0, 0))]
        + mod.specs(scale_i, d, row_col) + mod.specs(shift_i, d, row_col),
        out_specs=pl.BlockSpec((tm, d), lambda i: (i, 0)),
        out_shape=jax.ShapeDtypeStruct((n, d), BF16),
        compiler_params=pltpu.CompilerParams(dimension_semantics=("parallel",)),
        name="norm_mod",
    )(x, g.reshape(1, d), mod.m_seq, mod.m_tok, mod.m_seq, mod.m_tok)


def _moe_combine_kernel(pos_ref, x_ref, w_ref, g_seq_ref, g_tok_ref, ys_ref, o_ref, buf_ref, sem, *, mod):
    tm = mod.tm
    i = pl.program_id(0)
    base = i * (tm * TOP_K)

    def row_copy(t, k):
        return pltpu.make_async_copy(ys_ref.at[pl.ds(pos_ref[base + t * TOP_K + k], 1)],
                                     buf_ref.at[k, pl.ds(t, 1)], sem)

    def issue(t, carry):
        for k in range(TOP_K):
            row_copy(t, k).start()
        return carry

    def drain(t, carry):
        for k in range(TOP_K):
            row_copy(t, k).wait()
        return carry

    lax.fori_loop(0, tm, issue, 0)
    lax.fori_loop(0, tm, drain, 0)
    moe = w_ref[:, 0:1] * buf_ref[0]
    for k in range(1, TOP_K):
        moe = moe + w_ref[:, k:k + 1] * buf_ref[k]
    o_ref[...] = x_ref[...] + mod.pick(i, g_seq_ref, g_tok_ref) * moe


def _moe_combine(x, ys, pos, wts, mod, gate_i):
    n, d = x.shape
    tm = mod.tm
    assert n == mod.n_rows and pos.shape == (n * TOP_K,)
    tile = pl.BlockSpec((tm, d), lambda i, pos_ref: (i, 0))
    return pl.pallas_call(
        functools.partial(_moe_combine_kernel, mod=mod),
        grid_spec=pltpu.PrefetchScalarGridSpec(
            num_scalar_prefetch=1, grid=(n // tm,),
            in_specs=[tile, pl.BlockSpec((tm, TOP_K), lambda i, pos_ref: (i, 0))]
            + mod.specs(gate_i, d, lambda i, pos_ref: (i, 0))
            + [pl.BlockSpec(memory_space=pl.ANY)],
            out_specs=tile,
            scratch_shapes=[pltpu.VMEM((TOP_K, tm, d), F32), pltpu.SemaphoreType.DMA(())]),
        out_shape=jax.ShapeDtypeStruct((n, d), F32),
        compiler_params=pltpu.CompilerParams(dimension_semantics=("arbitrary",)),
        name="moe_combine",
    )(pos, x, wts, mod.m_seq, mod.m_tok, ys)


def _seg_sum(x, ones_blockdiag):
    hi = x.astype(BF16)
    lo = (x - hi.astype(F32)).astype(BF16)
    return (jnp.dot(hi, ones_blockdiag, preferred_element_type=F32)
            + jnp.dot(lo, ones_blockdiag, preferred_element_type=F32))


def _ones_blockdiag():
    same_head = ((lax.broadcasted_iota(jnp.int32, (PAIR, PAIR), 0) >= HEAD)
                 == (lax.broadcasted_iota(jnp.int32, (PAIR, PAIR), 1) >= HEAD))
    return same_head.astype(BF16)


def _prep_kernel(*refs, first_layer):
    if first_layer:
        (k_ref, v_ref, twd_ref, ad_ref, sgd_ref, wd2_ref, wa2_ref, wg2_ref, d0_ref, a0_ref, kk_w_ref,
         ka_ref, ld_o, kmod_o, v_o, kk_o, kka_o, g_o) = refs
    else:
        (k_ref, v_ref, twd_ref, ad_ref, sgd_ref, wd2_ref, wa2_ref, wg2_ref, d0_ref, a0_ref, kk_w_ref,
         ka_ref, vf_ref, vd_ref, wv2_ref, v0_ref, ld_o, kmod_o, v_o, kk_o, kka_o, g_o) = refs
    k = k_ref[...]
    y = -(d0_ref[...] + _dot_bf16(twd_ref, wd2_ref))
    softplus = jnp.maximum(y, 0.) + jnp.log(1. + jnp.exp(-jnp.abs(y)))
    ld_o[...] = -jnp.exp(-softplus - 0.5)
    a = jax.nn.sigmoid(a0_ref[...] + _dot_bf16(ad_ref, wa2_ref))
    g_o[...] = _dot_bf16(sgd_ref, wg2_ref)
    v = v_ref[...]
    if not first_layer:
        v = v + (vf_ref[...] - v) * jax.nn.sigmoid(v0_ref[...] + _dot_bf16(vd_ref, wv2_ref))
    v_o[...] = v
    kk = k * kk_w_ref[...]
    tm, wc = kk.shape
    nblk = wc // PAIR
    sq = jnp.concatenate([(kk * kk)[:, q * PAIR:(q + 1) * PAIR] for q in range(nblk)], axis=0)
    ss = _seg_sum(sq, _ones_blockdiag())
    ss = jnp.concatenate([ss[q * tm:(q + 1) * tm] for q in range(nblk)], axis=1)
    kk = kk / jnp.maximum(jnp.sqrt(ss), 1e-12)
    kk_o[...] = kk
    kka_o[...] = kk * a
    kmod_o[...] = k * (1 + (a - 1) * ka_ref[...])


def _prep(z, acts, v_first, weights, vecs, layer, *, tm, wc, k_off, v_off):
    n = z.shape[0]
    twd, ad, sgd, vd = acts
    w_d2, w_a2, w_g2, w_v2 = weights
    d0, a0, kk_w, ka, v0 = vecs
    rw = w_d2.shape[-1]
    first_layer = v_first is None
    col = lambda off: pl.BlockSpec((tm, wc), lambda i, j: (i, j + off))
    act = lambda a_: pl.BlockSpec((tm, a_.shape[1]), lambda i, j: (i, 0))
    lw = lambda w_, l_: pl.BlockSpec((None, w_.shape[1], wc), lambda i, j: (l_, 0, j))
    vec = lambda l_: pl.BlockSpec((None, 1, wc), lambda i, j: (l_, 0, j))
    args = [z, z, twd, ad, sgd, w_d2, w_a2, w_g2, d0, a0, kk_w, ka]
    specs = [col(k_off), col(v_off), act(twd), act(ad), act(sgd), lw(w_d2, layer), lw(w_a2, layer),
             lw(w_g2, layer), vec(layer), vec(layer), vec(layer), vec(layer)]
    if not first_layer:
        args += [v_first, vd, w_v2, v0]
        specs += [col(0), act(vd), lw(w_v2, layer - 1), vec(layer - 1)]
    out = jax.ShapeDtypeStruct((n, rw), F32)
    return pl.pallas_call(
        functools.partial(_prep_kernel, first_layer=first_layer),
        grid=(n // tm, rw // wc),
        in_specs=specs,
        out_specs=[col(0)] * 6,
        out_shape=[out] * 6,
        compiler_params=pltpu.CompilerParams(dimension_semantics=("parallel", "arbitrary")),
        name="rwkv_prep",
    )(*args)


def _wkv_kernel(r_ref, d_ref, k_ref, v_ref, kk_ref, kka_ref, s0_ref, o_ref, s_ref, st_ref, *, tc, npairs):
    @pl.when(pl.program_id(1) == 0)
    def _():
        for p in range(npairs):
            st_ref[p] = jnp.concatenate([s0_ref[0, 2 * p], s0_ref[0, 2 * p + 1]], axis=1)

    lane = lax.broadcasted_iota(jnp.int32, (HEAD, PAIR), 1)
    row = lax.broadcasted_iota(jnp.int32, (HEAD, PAIR), 0)
    diag = ((lane & (HEAD - 1)) == row).astype(F32)
    jr = lax.broadcasted_iota(jnp.int32, (PAIR, PAIR), 0) >= HEAD
    jc = lax.broadcasted_iota(jnp.int32, (PAIR, PAIR), 1) >= HEAD
    ones_blockdiag = (jr == jc).astype(BF16)

    rows = min(tc, SUBLANES)
    row_id = lax.broadcasted_iota(jnp.int32, (rows, PAIR), 0)

    def row_group(gi, carry):
        base = pl.multiple_of(gi * rows, rows)

        def rowvec(ref, p, i):
            return ref[0, pl.ds(base, rows), p * PAIR:(p + 1) * PAIR][i:i + 1]

        o_tiles = [jnp.zeros((rows, PAIR), F32)] * npairs
        for i in range(rows):
            sk = jnp.concatenate([st_ref[p] * rowvec(kk_ref, p, i) for p in range(npairs)], axis=0)
            vd = jnp.concatenate([diag * rowvec(v_ref, p, i) for p in range(npairs)], axis=0)
            y_kk = jnp.dot(sk.astype(BF16), ones_blockdiag, preferred_element_type=F32)
            y_v = _seg_sum(vd, ones_blockdiag)
            qs = []
            for p in range(npairs):
                s_kk = y_kk[p * HEAD:(p + 1) * HEAD]
                v_col = y_v[p * HEAD:(p + 1) * HEAD]
                s = (st_ref[p] * rowvec(d_ref, p, i) - s_kk * rowvec(kka_ref, p, i)
                     + v_col * rowvec(k_ref, p, i))
                st_ref[p] = s
                qs.append(s * rowvec(r_ref, p, i))
            z = jnp.dot(jnp.concatenate(qs, axis=0).astype(BF16), ones_blockdiag,
                        preferred_element_type=F32)
            for p in range(npairs):
                o_col = z[p * HEAD:(p + 1) * HEAD]
                o_row = jnp.sum(o_col * diag, axis=0, keepdims=True)
                o_tiles[p] = jnp.where(row_id == i, o_row, o_tiles[p])
        for p in range(npairs):
            o_ref[0, pl.ds(base, rows), p * PAIR:(p + 1) * PAIR] = o_tiles[p]
        return carry

    lax.fori_loop(0, tc // rows, row_group, 0)

    @pl.when(pl.program_id(1) == pl.num_programs(1) - 1)
    def _():
        for p in range(npairs):
            s = st_ref[p]
            s_ref[0, 2 * p] = s[:, :HEAD]
            s_ref[0, 2 * p + 1] = s[:, HEAD:]


def _wkv(r, decay, k, v, kk, kka, s0_layers, layer, *, tc):
    b, t, w = r.shape
    npairs = w // PAIR
    assert t % tc == 0
    seq = pl.BlockSpec((1, tc, w), lambda i, c: (i, c, 0))
    st = pl.BlockSpec((1, 2 * npairs, HEAD, HEAD), lambda i, c: (i, 0, 0, 0))
    st_in = pl.BlockSpec((None, 1, 2 * npairs, HEAD, HEAD), lambda i, c: (layer, i, 0, 0, 0))
    return pl.pallas_call(
        functools.partial(_wkv_kernel, tc=tc, npairs=npairs),
        grid=(b, t // tc),
        in_specs=[seq] * 6 + [st_in],
        out_specs=[seq, st],
        out_shape=[jax.ShapeDtypeStruct((b, t, w), F32),
                   jax.ShapeDtypeStruct((b, 2 * npairs, HEAD, HEAD), F32)],
        scratch_shapes=[pltpu.VMEM((npairs, HEAD, PAIR), F32)],
        compiler_params=pltpu.CompilerParams(dimension_semantics=("parallel", "arbitrary")),
        name="wkv_steps",
    )(r, decay, k, v, kk, kka, s0_layers)


def _bdot(x, y):
    return jnp.dot(x.astype(BF16), y.astype(BF16), preferred_element_type=F32)


def _bdot_nt(x, y):
    return lax.dot_general(x.astype(BF16), y.astype(BF16), (((1,), (1,)), ((), ())),
                           preferred_element_type=F32)


def _split3(x):
    h1 = x.astype(BF16)
    r1 = x - h1.astype(F32)
    h2 = r1.astype(BF16)
    return h1, h2, (r1 - h2.astype(F32)).astype(BF16)


def _wkv_chunk_kernel(r_ref, ld_ref, k_ref, v_ref, kk_ref, kka_ref, g_ref, rk_ref, lnw_ref, lnb_ref,
                      s0_ref, o_ref, s_ref, *, npairs):
    c = HEAD

    @pl.when(pl.program_id(2) == 0)
    def _():
        s_ref[...] = s0_ref[...]

    row = lax.broadcasted_iota(jnp.int32, (c, PAIR), 0)
    col = lax.broadcasted_iota(jnp.int32, (c, PAIR), 1)
    left = col < c
    strict = (col & (c - 1)) < row
    incl = (col & (c - 1)) <= row
    tri = (lax.broadcasted_iota(jnp.int32, (c, c), 1)
           <= lax.broadcasted_iota(jnp.int32, (c, c), 0)).astype(BF16)
    same_head = ((lax.broadcasted_iota(jnp.int32, (PAIR, PAIR), 0) >= HEAD)
                 == (lax.broadcasted_iota(jnp.int32, (PAIR, PAIR), 1) >= HEAD))
    ones_blockdiag = same_head.astype(BF16)

    def halves(x):
        return jnp.concatenate([jnp.where(left, x, 0.), jnp.where(left, 0., x)], axis=1)

    def pairs(x):
        return jnp.stack([x[:, p * PAIR:(p + 1) * PAIR] for p in range(npairs)])

    def bmm(x, y):
        return lax.dot_general(x.astype(BF16), y.astype(BF16), (((2,), (1,)), ((0,), (0,))),
                               preferred_element_type=F32)

    def bmm_nt(x, y):
        return lax.dot_general(x.astype(BF16), y.astype(BF16), (((2,), (2,)), ((0,), (0,))),
                               preferred_element_type=F32)

    def seg_sum(x):
        return jnp.dot(x.reshape(npairs * c, PAIR).astype(BF16), ones_blockdiag,
                       preferred_element_type=F32).reshape(npairs, c, PAIR)

    ld = ld_ref[0]
    r, k, v = r_ref[0], k_ref[0], v_ref[0]
    b = sum(jnp.dot(tri, part, preferred_element_type=F32) for part in _split3(ld))
    eb = jnp.exp(b)
    e_inv = jnp.exp(-b)
    a_t = pairs(-kk_ref[0] * jnp.exp(b - ld))
    r_t = pairs(r * eb)
    k_t = pairs(k * e_inv)
    b_t = pairs(kka_ref[0] * e_inv)
    eb_last = pairs(eb[c - 1:c])
    v_p = pairs(v)
    s = s_ref[0]

    gram = bmm_nt(jnp.concatenate([a_t, r_t], axis=1),
                  jnp.concatenate([halves(b_t), halves(k_t)], axis=1))
    l_ab = jnp.where(strict, gram[:, :c, :PAIR], 0.)
    l_ak = jnp.where(strict, gram[:, :c, PAIR:], 0.)
    m_rb = jnp.where(incl, gram[:, c:, :PAIR], 0.)
    m_rk = jnp.where(incl, gram[:, c:, PAIR:], 0.)

    v_h = halves(v_p)
    u = bmm_nt(a_t, s) + bmm(l_ak, v_h)
    l_pow = l_ab
    n = 1
    while True:
        u = u + bmm(l_pow, halves(u))
        n *= 2
        if n >= c:
            break
        l_pow = bmm(l_pow, halves(l_pow))
    o = bmm_nt(r_t, s) + bmm(jnp.concatenate([m_rk, m_rb], axis=2),
                             jnp.concatenate([v_h, halves(u)], axis=1))
    vu = jnp.concatenate([v_p, u], axis=1)
    vu_t = jnp.stack([vu[p].T for p in range(npairs)])
    s_new = bmm(vu_t, jnp.concatenate([k_t, b_t], axis=1))
    s_ref[0] = (s + jnp.where(same_head, s_new, 0.)) * eb_last

    mean = seg_sum(o) * (1.0 / HEAD)
    dev = o - mean
    var = seg_sum(dev * dev) * (1.0 / HEAD)
    bonus = seg_sum(pairs(r * k * rk_ref[...])) * v_p
    o = dev * lax.rsqrt(var + GN_EPS) * pairs(lnw_ref[...]) + pairs(lnb_ref[...]) + bonus
    o = o * pairs(g_ref[0])
    for p in range(npairs):
        o_ref[0, :, p * PAIR:(p + 1) * PAIR] = o[p].astype(o_ref.dtype)


def _blockdiag_state(s):
    b, h = s.shape[:2]
    s = s.reshape(b, h // 2, 2, HEAD, HEAD)
    z = jnp.zeros_like(s[:, :, 0])
    return jnp.concatenate([jnp.concatenate([s[:, :, 0], z], axis=-1),
                            jnp.concatenate([z, s[:, :, 1]], axis=-1)], axis=-2)


def _unblockdiag_state(s):
    b, hp = s.shape[:2]
    return jnp.stack([s[:, :, :HEAD, :HEAD], s[:, :, HEAD:, HEAD:]], axis=2).reshape(
        b, 2 * hp, HEAD, HEAD)


def _wkv_chunked(r, log_decay, k, v, kk, kka, g, r_k, ln_w, ln_b, s0, *, npairs, r_off=0):
    b, t, w = log_decay.shape
    wb = npairs * PAIR
    assert t % HEAD == 0 and (w // PAIR) % npairs == 0 and r_off % wb == 0
    seq = pl.BlockSpec((1, HEAD, wb), lambda i, j, c: (i, c, j))
    r_seq = pl.BlockSpec((1, HEAD, wb), lambda i, j, c: (i, c, j + r_off // wb))
    vec = pl.BlockSpec((1, wb), lambda i, j, c: (0, j))
    st = pl.BlockSpec((1, npairs, PAIR, PAIR), lambda i, j, c: (i, j, 0, 0))
    o, s = pl.pallas_call(
        functools.partial(_wkv_chunk_kernel, npairs=npairs),
        grid=(b, w // wb, t // HEAD),
        in_specs=[r_seq] + [seq] * 6 + [vec] * 3 + [st],
        out_specs=[seq, st],
        out_shape=[jax.ShapeDtypeStruct((b, t, w), BF16),
                   jax.ShapeDtypeStruct((b, w // PAIR, PAIR, PAIR), F32)],
        compiler_params=pltpu.CompilerParams(
            dimension_semantics=("parallel", "parallel", "arbitrary")),
        name="wkv_chunked",
    )(r, log_decay, k, v, kk, kka, g, r_k.reshape(1, w), ln_w.reshape(1, w), ln_b.reshape(1, w),
      _blockdiag_state(s0))
    return o, _unblockdiag_state(s)


def _moe_up_kernel(te_ref, tv_ref, x_ref, wg_ref, wu_ref, h_ref):
    i = pl.program_id(0)

    @pl.when(tv_ref[i] > 0)
    def _():
        x = x_ref[...]
        g = jnp.dot(x, wg_ref[...].astype(BF16), preferred_element_type=F32)
        u = jnp.dot(x, wu_ref[...].astype(BF16), preferred_element_type=F32)
        h_ref[...] = (g * jax.nn.sigmoid(g) * u).astype(h_ref.dtype)

    @pl.when(tv_ref[i] == 0)
    def _():
        h_ref[...] = jnp.zeros_like(h_ref)


def _moe_down_kernel(te_ref, tv_ref, h_ref, wd_ref, y_ref):
    i = pl.program_id(0)

    @pl.when(tv_ref[i] > 0)
    def _():
        y_ref[...] = jnp.dot(h_ref[...], wd_ref[...].astype(BF16), preferred_element_type=F32)

    @pl.when(tv_ref[i] == 0)
    def _():
        y_ref[...] = jnp.zeros_like(y_ref)


def _moe_experts(xs, tile_expert, tile_valid, w_gate, w_up, w_down, layer):
    p, d = xs.shape
    de = w_gate.shape[-1]
    nt = p // MOE_TM
    hid = pl.pallas_call(
        _moe_up_kernel,
        grid_spec=pltpu.PrefetchScalarGridSpec(
            num_scalar_prefetch=2, grid=(nt,),
            in_specs=[pl.BlockSpec((MOE_TM, d), lambda i, te, tv: (i, 0)),
                      pl.BlockSpec((None, None, d, de), lambda i, te, tv: (layer, te[i], 0, 0)),
                      pl.BlockSpec((None, None, d, de), lambda i, te, tv: (layer, te[i], 0, 0))],
            out_specs=pl.BlockSpec((MOE_TM, de), lambda i, te, tv: (i, 0))),
        out_shape=jax.ShapeDtypeStruct((p, de), BF16),
        compiler_params=pltpu.CompilerParams(dimension_semantics=("arbitrary",)),
        name="moe_up",
    )(tile_expert, tile_valid, xs, w_gate, w_up)
    return pl.pallas_call(
        _moe_down_kernel,
        grid_spec=pltpu.PrefetchScalarGridSpec(
            num_scalar_prefetch=2, grid=(nt,),
            in_specs=[pl.BlockSpec((MOE_TM, de), lambda i, te, tv: (i, 0)),
                      pl.BlockSpec((None, None, de, d), lambda i, te, tv: (layer, te[i], 0, 0))],
            out_specs=pl.BlockSpec((MOE_TM, d), lambda i, te, tv: (i, 0))),
        out_shape=jax.ShapeDtypeStruct((p, d), F32),
        compiler_params=pltpu.CompilerParams(dimension_semantics=("arbitrary",)),
        name="moe_down",
    )(tile_expert, tile_valid, hid, w_down)


def _hier_moe(h2, layer, w_route, b_route_group, b_route_expert, w_gate, w_up, w_down, *, tm_route):
    n, d = h2.shape
    n_groups = b_route_group.shape[-1]
    n_experts = b_route_expert.shape[-1]
    logits = _mm(h2, w_route, tm=tm_route, tn=w_route.shape[-1], b_lead=(layer,))
    p_group = jax.nn.softmax(logits[:, :n_groups] + b_route_group[layer], axis=-1)
    g_idx = jnp.argmax(p_group, axis=-1, keepdims=True)
    p_top = jnp.max(p_group, axis=-1, keepdims=True)
    le = (logits[:, n_groups:n_groups + n_experts] + b_route_expert[layer])
    le = le.reshape(n, n_groups, EXP_PER_GROUP)
    le = jnp.take_along_axis(le, g_idx[:, :, None], axis=1)[:, 0]
    v_top, e_idx, rest = [], [], le
    for _ in range(TOP_K):
        best = jnp.argmax(rest, axis=-1, keepdims=True)
        v_top.append(jnp.max(rest, axis=-1, keepdims=True))
        e_idx.append(best)
        rest = jnp.where(jnp.arange(EXP_PER_GROUP) == best, -jnp.inf, rest)
    v_top, e_idx = jnp.concatenate(v_top, axis=-1), jnp.concatenate(e_idx, axis=-1)
    wts = p_top * jax.nn.softmax(v_top, axis=-1)
    gidx = (g_idx * EXP_PER_GROUP + e_idx).astype(jnp.int32)

    na = n * TOP_K
    nt = na // MOE_TM + n_experts
    e_flat = gidx.reshape(na)
    order = jnp.argsort(e_flat, stable=True).astype(jnp.int32)
    sorted_e = e_flat[order]
    counts = jnp.zeros((n_experts,), jnp.int32).at[e_flat].add(1)
    tiles_per = (counts + MOE_TM - 1) // MOE_TM
    tile_end = jnp.cumsum(tiles_per)
    pad_start = (tile_end - tiles_per) * MOE_TM
    count_start = jnp.cumsum(counts) - counts
    dest = pad_start[sorted_e] + jnp.arange(na, dtype=jnp.int32) - count_start[sorted_e]
    row_token = jnp.zeros((nt * MOE_TM,), jnp.int32).at[dest].set(order // TOP_K)
    pos = jnp.zeros((na,), jnp.int32).at[order].set(dest)
    tile_ids = jnp.arange(nt, dtype=jnp.int32)
    tile_expert = jnp.minimum(jnp.searchsorted(tile_end, tile_ids, side='right'),
                              n_experts - 1).astype(jnp.int32)
    tile_valid = (tile_ids < tile_end[-1]).astype(jnp.int32)

    xs = jnp.take(h2, row_token, axis=0, mode="clip")
    ys = _moe_experts(xs, tile_expert, tile_valid, w_gate, w_up, w_down, layer)
    return ys, pos, wts


def _rms_norm(x, g):
    return x * lax.rsqrt(jnp.mean(x * x, axis=-1, keepdims=True) + RMS_EPS) * g


def _pool_diffs(u, u_past, start_pos):
    b, t, pw = u.shape
    grp = pw // len(POOL_WINDOWS)
    ext = jnp.concatenate([u_past, u], axis=1)
    pos = start_pos + jnp.arange(t)
    diffs = []
    for gi, win in enumerate(POOL_WINDOWS):
        e = ext[:, :, gi * grp:(gi + 1) * grp]
        win_sum = e[:, POOL_BUF:POOL_BUF + t]
        for i in range(1, win):
            win_sum = win_sum + e[:, POOL_BUF - i:POOL_BUF - i + t]
        cnt = jnp.minimum(win, pos + 1).astype(F32)[None, :, None]
        diffs.append(win_sum / cnt - e[:, POOL_BUF:])
    return jnp.concatenate(diffs, axis=-1).reshape(b * t, pw), ext[:, t:]


def kernel(x_prompt, x_sample, state_shift, state_pool, state_wkv, c_prompt, c_sample, w_ada, b_ada, g_mix, g_ffn, g_final, w_in, w_vres_down, mu, mu_vres, w_pool, pool_scale, w_decay0, w_decay2, w_a0, w_a2, w_g2, w_v0, w_v2, k_k, k_a, r_k, ln_x_w, ln_x_b, w_out, w_route_group, b_route_group, w_route_expert, b_route_expert, w_exp_gate, w_exp_up, w_exp_down):
    bp, tp, d = x_prompt.shape
    bs, ts, _ = x_sample.shape
    depth = w_in.shape[0]
    n_heads = state_wkv.shape[2]
    rw = n_heads * HEAD
    pw = state_pool.shape[-1]
    n_mod = w_ada.shape[1] // d
    n_p, n_s = bp * tp, bs * ts
    n_tok = n_p + n_s
    c_main = pw + 3 * rw
    n_tail = w_in.shape[-1] - c_main
    v_lora = w_vres_down.shape[-1]
    w_lora, a_lora = w_decay2.shape[1], w_a2.shape[1]

    tm_tok = n_tok // 8
    tm_p = tp // 2
    tm_norm = math.gcd(256, tp, n_s)
    tm_prep = tm_norm
    tm_mix = math.gcd(512, tp, n_s)
    npairs = max(q for q in range(1, WKV_PAIRS + 1)
                 if (n_heads // 2) % q == 0 and pw % (q * PAIR) == 0)

    def stream(a_p, a_s):
        return jnp.concatenate([a_p.reshape(n_p, -1), a_s.reshape(n_s, -1)], axis=0)

    c_all = jnp.concatenate([c_prompt, c_sample], axis=0)
    rows_c = -(-c_all.shape[0] // 16) * 16
    c_act = jnp.pad(jax.nn.silu(c_all), ((0, rows_c - c_all.shape[0]), (0, 0))).astype(BF16)
    mod = _mm(c_act, w_ada, tm=rows_c, tn=512)[:bp + bs].reshape(bp + bs, n_mod, d)

    zeros_l = lambda *shape: jnp.zeros((1,) + shape, F32)
    w_tail = jnp.concatenate(
        [w_in[:, :, c_main:], jnp.concatenate([zeros_l(d, v_lora), w_vres_down], axis=0)], axis=-1)
    mu_main = jnp.concatenate([jnp.zeros((depth, pw), F32), mu[:, :3 * rw]], axis=-1)[:, None]
    mu_tail = jnp.concatenate(
        [mu[:, 3 * rw:], jnp.concatenate([zeros_l(v_lora), mu_vres], axis=0)], axis=-1)[:, None]
    route_pad = 128 - (w_route_group.shape[-1] + w_route_expert.shape[-1])
    w_route = jnp.concatenate(
        [w_route_group, w_route_expert, jnp.zeros((depth, d, route_pad), F32)], axis=-1)
    pool_scale3 = pool_scale[:, None]

    lora_weights = (w_decay2, w_a2, w_g2, w_v2)
    lora_vecs = (w_decay0[:, None], w_a0[:, None], k_k[:, None], k_a[:, None], w_v0[:, None])
    assert rw % pw == 0 and pw % (npairs * PAIR) == 0

    def mixer_inputs(z, zt, v_first, l, tm):
        acts = (jnp.tanh(zt[:, :w_lora]).astype(BF16),
                zt[:, w_lora:w_lora + a_lora].astype(BF16),
                jax.nn.sigmoid(zt[:, w_lora + a_lora:n_tail]).astype(BF16),
                zt[:, n_tail:].astype(BF16))
        return _prep(z, acts, v_first, lora_weights, lora_vecs, l, tm=tm, wc=pw,
                     k_off=1 + rw // pw, v_off=1 + 2 * (rw // pw))

    x = stream(x_prompt, x_sample)
    vf_p = vf_s = None
    shift_p, pool_p, wkv_p, shift_s, pool_s, wkv_s = [], [], [], [], [], []
    zero_state = jnp.zeros((bp, n_heads, HEAD, HEAD), F32)
    for l in range(depth):
        m = mod + b_ada[l]
        m_seq = m[:bp].transpose(1, 0, 2)[:, :, None]
        m_tok = jnp.repeat(m[bp:], ts, axis=0).transpose(1, 0, 2)
        x_last = jnp.concatenate([x[tp - 1:n_p:tp], x[n_p + ts - 1::ts]])
        h_last = _rms_norm(x_last, g_mix[l]) * (1 + m[:, 1]) + m[:, 0]
        shift_p.append(h_last[:bp])
        shift_s.append(h_last[bp:])
        rows256 = _RowMod(m_seq, m_tok, tp, tm_norm)
        rows512 = _RowMod(m_seq, m_tok, tp, tm_mix)
        h_bf = _norm_mod(x, g_mix[l], rows256, 1, 0)
        hs_bf = jnp.concatenate([state_shift[l][:, None].astype(BF16), h_bf[n_p:].reshape(bs, ts, d)],
                                axis=1).reshape(bs * (ts + 1), d)

        z_p = _mm_lerp(h_bf, w_in, mu_main, l, tm=tm_p, tn=512, m=n_p, n=c_main,
                       tiles_per_seq=tp // tm_p)
        zt_p = _mm_lerp(h_bf, w_tail, mu_tail, l, tm=tm_p, tn=n_tail + v_lora, m=n_p,
                        n=n_tail + v_lora, tiles_per_seq=tp // tm_p)
        drop_first = lambda a_: a_.reshape(bs, ts + 1, -1)[:, 1:].reshape(n_s, -1)
        z_s = drop_first(_mm_lerp(hs_bf, w_in, mu_main, l, tm=hs_bf.shape[0], tn=512,
                                  m=hs_bf.shape[0], n=c_main, tiles_per_seq=None))
        zt_s = drop_first(_mm_lerp(hs_bf, w_tail, mu_tail, l, tm=hs_bf.shape[0],
                                   tn=n_tail + v_lora, m=hs_bf.shape[0], n=n_tail + v_lora,
                                   tiles_per_seq=None))

        d_p, np_p = _pool_diffs(z_p[:, :pw].reshape(bp, tp, pw), jnp.zeros((bp, POOL_BUF, pw), F32), 0)
        d_s, np_s = _pool_diffs(z_s[:, :pw].reshape(bs, ts, pw), state_pool[l], PAST_LEN)
        pool_p.append(np_p)
        pool_s.append(np_s)
        pool_out = _pool_mm(jnp.concatenate([d_p, d_s], axis=0).astype(BF16), w_pool, pool_scale3,
                            l, tm=tm_tok)

        in_p = mixer_inputs(z_p, zt_p, vf_p, l, tm_prep)
        in_s = mixer_inputs(z_s, zt_s, vf_s, l, tm_prep)
        if l == 0:
            vf_p, vf_s = in_p[2], in_s[2]
        seq_p = lambda a_: a_.reshape(bp, tp, -1)
        seq_s = lambda a_: a_.reshape(bs, ts, rw)
        o_p, s_p = _wkv_chunked(seq_p(z_p), *[seq_p(a_) for a_ in in_p], r_k[l], ln_x_w[l], ln_x_b[l],
                                zero_state, npairs=npairs, r_off=pw)
        ld_s, k_s, v_s, kk_s, kka_s, g_s = in_s
        r_s = z_s[:, pw:pw + rw]
        o_s, s_s = _wkv(seq_s(r_s), seq_s(jnp.exp(ld_s)), seq_s(k_s), seq_s(v_s), seq_s(kk_s),
                        seq_s(kka_s), state_wkv, l, tc=ts)
        wkv_p.append(s_p)
        wkv_s.append(s_s)
        heads = lambda a_: a_.reshape(n_s, n_heads, HEAD)
        o_s = heads(o_s)
        mean = jnp.mean(o_s, axis=-1, keepdims=True)
        var = jnp.mean(jnp.square(o_s - mean), axis=-1, keepdims=True)
        o_s = ((o_s - mean) * lax.rsqrt(var + GN_EPS)).reshape(n_s, rw) * ln_x_w[l] + ln_x_b[l]
        bonus = jnp.sum(heads(r_s) * heads(k_s) * r_k[l], axis=-1, keepdims=True) * heads(v_s)
        o_s = ((o_s + bonus.reshape(n_s, rw)) * g_s).astype(BF16)
        o = jnp.concatenate([o_p.reshape(n_p, rw), o_s], axis=0)
        x = _mix_out(pool_out, o, w_out, x, rows512, 2, l, tn=512)

        h2 = _norm_mod(x, g_ffn[l], rows256, 4, 3)
        ys, pos, wts = _hier_moe(h2, l, w_route, b_route_group, b_route_expert,
                                 w_exp_gate, w_exp_up, w_exp_down, tm_route=tm_tok)
        x = _moe_combine(x, ys, pos, wts, rows256, 5)

    y = _rms_norm(x, g_final)
    return (y[:n_p].reshape(bp, tp, d), y[n_p:].reshape(bs, ts, d),
            jnp.stack(shift_p), jnp.stack(pool_p), jnp.stack(wkv_p),
            jnp.stack(shift_s), jnp.stack(pool_s), jnp.stack(wkv_s))
```

```python
import functools
import math

import jax
import jax.numpy as jnp
from jax import lax
from jax.experimental import pallas as pl
from jax.experimental.pallas import tpu as pltpu

F32 = jnp.float32
BF16 = jnp.bfloat16

HEAD = 64
PAIR = 2 * HEAD
SUBLANES = 8
POOL_WINDOWS = (2, 4, 8, 16)
POOL_BUF = max(POOL_WINDOWS) - 1
EXP_PER_GROUP = 8
TOP_K = 2
RMS_EPS = 1e-6
GN_EPS = 64e-5
MOE_TM = 256
PAST_LEN = 16384
WKV_PAIRS = 8


def _dot_bf16(a_ref, b_ref):
    return jnp.dot(a_ref[...].astype(BF16), b_ref[...].astype(BF16), preferred_element_type=F32)


def _mm_kernel(a_ref, b_ref, o_ref):
    o_ref[...] = _dot_bf16(a_ref, b_ref).astype(o_ref.dtype)


def _mm(a, b, *, tm, tn, b_lead=(), n=None, m=None, out_dtype=F32):
    k = a.shape[1]
    m = a.shape[0] if m is None else m
    n = b.shape[-1] if n is None else n
    assert b.shape[-2] == k and m % tm == 0 and n % tn == 0, (a.shape, b.shape, m, tm, n, tn)
    lead = tuple(b_lead)
    return pl.pallas_call(
        _mm_kernel,
        grid=(m // tm, n // tn),
        in_specs=[pl.BlockSpec((tm, k), lambda i, j: (i, 0)),
                  pl.BlockSpec((None,) * len(lead) + (k, tn), lambda i, j: lead + (0, j))],
        out_specs=pl.BlockSpec((tm, tn), lambda i, j: (i, j)),
        out_shape=jax.ShapeDtypeStruct((m, n), out_dtype),
        compiler_params=pltpu.CompilerParams(dimension_semantics=("parallel", "arbitrary")),
        name="mm",
    )(a, b)


def _mm_lerp_kernel(a_ref, b_ref, mu_ref, o_ref, carry_ref, *, tiles_per_seq):
    i, j = pl.program_id(0), pl.program_id(1)
    acc = _dot_bf16(a_ref, b_ref)
    prev = pltpu.roll(acc, 1, axis=0)
    if tiles_per_seq is not None:
        @pl.when(i == 0)
        def _():
            carry_ref[j] = jnp.zeros(carry_ref.shape[1:], F32)

        row = lax.broadcasted_iota(jnp.int32, acc.shape, 0)
        last = jnp.where(i % tiles_per_seq == 0, 0., carry_ref[j][SUBLANES - 1:])
        prev = jnp.where(row == 0, last, prev)
        carry_ref[j] = acc[acc.shape[0] - SUBLANES:]
    o_ref[...] = acc + (prev - acc) * mu_ref[...]


def _mm_lerp(a, b, mu_cols, layer, *, tm, tn, m, n, tiles_per_seq):
    k = a.shape[1]
    assert m % tm == 0 and n % tn == 0
    return pl.pallas_call(
        functools.partial(_mm_lerp_kernel, tiles_per_seq=tiles_per_seq),
        grid=(m // tm, n // tn),
        in_specs=[pl.BlockSpec((tm, k), lambda i, j: (i, 0)),
                  pl.BlockSpec((None, k, tn), lambda i, j: (layer, 0, j)),
                  pl.BlockSpec((None, 1, tn), lambda i, j: (layer, 0, j))],
        out_specs=pl.BlockSpec((tm, tn), lambda i, j: (i, j)),
        out_shape=jax.ShapeDtypeStruct((m, n), F32),
        scratch_shapes=[pltpu.VMEM((n // tn, SUBLANES, tn), F32)],
        compiler_params=pltpu.CompilerParams(dimension_semantics=("arbitrary", "arbitrary")),
        name="proj_lerp",
    )(a, b, mu_cols)


def _pool_mm_kernel(a_ref, b_ref, s_ref, o_ref):
    o_ref[...] = (_dot_bf16(a_ref, b_ref) * s_ref[...]).astype(o_ref.dtype)


def _pool_mm(d, w_pool, scale, layer, *, tm):
    m = d.shape[0]
    g, c = w_pool.shape[1], w_pool.shape[2]
    return pl.pallas_call(
        _pool_mm_kernel,
        grid=(m // tm, g),
        in_specs=[pl.BlockSpec((tm, c), lambda i, j: (i, j)),
                  pl.BlockSpec((None, None, c, c), lambda i, j: (layer, j, 0, 0)),
                  pl.BlockSpec((None, 1, c), lambda i, j: (layer, 0, j))],
        out_specs=pl.BlockSpec((tm, c), lambda i, j: (i, j)),
        out_shape=jax.ShapeDtypeStruct((m, g * c), BF16),
        compiler_params=pltpu.CompilerParams(dimension_semantics=("parallel", "arbitrary")),
        name="pool_mm",
    )(d, w_pool, scale)


class _RowMod:
    def __init__(self, m_seq, m_tok, rows_per_seq, tm):
        self.m_seq, self.m_tok, self.tm = m_seq, m_tok, tm
        self.n_seq = m_seq.shape[1]
        assert rows_per_seq % tm == 0 and m_tok.shape[1] % tm == 0
        self.tiles_per_seq = rows_per_seq // tm
        self.seq_tiles = self.n_seq * self.tiles_per_seq
        self.n_rows = self.n_seq * rows_per_seq + m_tok.shape[1]

    def specs(self, comp, tn, row_col):
        def seq_map(*ids):
            i, j = row_col(*ids)
            return comp, jnp.minimum(i // self.tiles_per_seq, self.n_seq - 1), 0, j

        def tok_map(*ids):
            i, j = row_col(*ids)
            return comp, jnp.maximum(i - self.seq_tiles, 0), j

        return [pl.BlockSpec((None, None, 1, tn), seq_map), pl.BlockSpec((None, self.tm, tn), tok_map)]

    def pick(self, row_tile, seq_ref, tok_ref):
        return jnp.where(row_tile < self.seq_tiles, seq_ref[...], tok_ref[...])


def _mix_out_kernel(p_ref, o1_ref, o2_ref, o3_ref, w0_ref, w1_ref, w2_ref, w3_ref, x_ref,
                    g_seq_ref, g_tok_ref, y_ref, *, mod):
    acc = (_dot_bf16(p_ref, w0_ref) + _dot_bf16(o1_ref, w1_ref)
           + _dot_bf16(o2_ref, w2_ref) + _dot_bf16(o3_ref, w3_ref))
    y_ref[...] = x_ref[...] + mod.pick(pl.program_id(1), g_seq_ref, g_tok_ref) * acc


def _mix_out(pool_out, o, w_out, x, mod, gate_i, layer, *, tn):
    n, d = x.shape
    tm = mod.tm
    q = pool_out.shape[1]
    assert o.shape[1] == 3 * q and d == 4 * q and n == mod.n_rows
    w4 = w_out.reshape(w_out.shape[0], 4, q, d)
    a_spec = lambda c: pl.BlockSpec((tm, q), lambda j, i: (i, c))
    w_spec = lambda c: pl.BlockSpec((None, None, q, tn), lambda j, i: (layer, c, 0, j))
    tile = pl.BlockSpec((tm, tn), lambda j, i: (i, j))
    return pl.pallas_call(
        functools.partial(_mix_out_kernel, mod=mod),
        grid=(d // tn, n // tm),
        in_specs=[a_spec(0), a_spec(0), a_spec(1), a_spec(2),
                  w_spec(0), w_spec(1), w_spec(2), w_spec(3), tile]
        + mod.specs(gate_i, tn, lambda j, i: (i, j)),
        out_specs=tile,
        out_shape=jax.ShapeDtypeStruct((n, d), F32),
        compiler_params=pltpu.CompilerParams(dimension_semantics=("parallel", "arbitrary")),
        name="mix_out",
    )(pool_out, o, o, o, w4, w4, w4, w4, x, mod.m_seq, mod.m_tok)


def _norm_mod_kernel(x_ref, g_ref, sc_seq_ref, sc_tok_ref, sh_seq_ref, sh_tok_ref, o_ref, *, mod):
    x = x_ref[...]
    y = x * lax.rsqrt(jnp.mean(x * x, axis=-1, keepdims=True) + RMS_EPS) * g_ref[...]
    i = pl.program_id(0)
    scale = mod.pick(i, sc_seq_ref, sc_tok_ref)
    shift = mod.pick(i, sh_seq_ref, sh_tok_ref)
    o_ref[...] = (y * (1 + scale) + shift).astype(o_ref.dtype)


def _norm_mod(x, g, mod, scale_i, shift_i):
    n, d = x.shape
    tm = mod.tm
    assert n == mod.n_rows
    row_col = lambda i: (i, 0)
    return pl.pallas_call(
        functools.partial(_norm_mod_kernel, mod=mod),
        grid=(n // tm,),
        in_specs=[pl.BlockSpec((tm, d), lambda i: (i, 0)), pl.BlockSpec((1, d), lambda i: (0, 0))]
        + mod.specs(scale_i, d, row_col) + mod.specs(shift_i, d, row_col),
        out_specs=pl.BlockSpec((tm, d), lambda i: (i, 0)),
        out_shape=jax.ShapeDtypeStruct((n, d), BF16),
        compiler_params=pltpu.CompilerParams(dimension_semantics=("parallel",)),
        name="norm_mod",
    )(x, g.reshape(1, d), mod.m_seq, mod.m_tok, mod.m_seq, mod.m_tok)


def _moe_combine_kernel(pos_ref, x_ref, w_ref, g_seq_ref, g_tok_ref, ys_ref, o_ref, buf_ref, sem, *, mod):
    tm = mod.tm
    i = pl.program_id(0)
    base = i * (tm * TOP_K)

    def row_copy(t, k):
        return pltpu.make_async_copy(ys_ref.at[pl.ds(pos_ref[base + t * TOP_K + k], 1)],
                                     buf_ref.at[k, pl.ds(t, 1)], sem)

    def issue(t, carry):
        for k in range(TOP_K):
            row_copy(t, k).start()
        return carry

    def drain(t, carry):
        for k in range(TOP_K):
            row_copy(t, k).wait()
        return carry

    lax.fori_loop(0, tm, issue, 0)
    lax.fori_loop(0, tm, drain, 0)
    moe = w_ref[:, 0:1] * buf_ref[0]
    for k in range(1, TOP_K):
        moe = moe + w_ref[:, k:k + 1] * buf_ref[k]
    o_ref[...] = x_ref[...] + mod.pick(i, g_seq_ref, g_tok_ref) * moe


def _moe_combine(x, ys, pos, wts, mod, gate_i):
    n, d = x.shape
    tm = mod.tm
    assert n == mod.n_rows and pos.shape == (n * TOP_K,)
    tile = pl.BlockSpec((tm, d), lambda i, pos_ref: (i, 0))
    return pl.pallas_call(
        functools.partial(_moe_combine_kernel, mod=mod),
        grid_spec=pltpu.PrefetchScalarGridSpec(
            num_scalar_prefetch=1, grid=(n // tm,),
            in_specs=[tile, pl.BlockSpec((tm, TOP_K), lambda i, pos_ref: (i, 0))]
            + mod.specs(gate_i, d, lambda i, pos_ref: (i, 0))
            + [pl.BlockSpec(memory_space=pl.ANY)],
            out_specs=tile,
            scratch_shapes=[pltpu.VMEM((TOP_K, tm, d), F32), pltpu.SemaphoreType.DMA(())]),
        out_shape=jax.ShapeDtypeStruct((n, d), F32),
        compiler_params=pltpu.CompilerParams(dimension_semantics=("arbitrary",)),
        name="moe_combine",
    )(pos, x, wts, mod.m_seq, mod.m_tok, ys)


def _seg_sum(x, ones_blockdiag):
    hi = x.astype(BF16)
    lo = (x - hi.astype(F32)).astype(BF16)
    return (jnp.dot(hi, ones_blockdiag, preferred_element_type=F32)
            + jnp.dot(lo, ones_blockdiag, preferred_element_type=F32))


def _ones_blockdiag():
    same_head = ((lax.broadcasted_iota(jnp.int32, (PAIR, PAIR), 0) >= HEAD)
                 == (lax.broadcasted_iota(jnp.int32, (PAIR, PAIR), 1) >= HEAD))
    return same_head.astype(BF16)


def _prep_kernel(*refs, first_layer):
    if first_layer:
        (k_ref, v_ref, twd_ref, ad_ref, sgd_ref, wd2_ref, wa2_ref, wg2_ref, d0_ref, a0_ref, kk_w_ref,
         ka_ref, ld_o, kmod_o, v_o, kk_o, kka_o, g_o) = refs
    else:
        (k_ref, v_ref, twd_ref, ad_ref, sgd_ref, wd2_ref, wa2_ref, wg2_ref, d0_ref, a0_ref, kk_w_ref,
         ka_ref, vf_ref, vd_ref, wv2_ref, v0_ref, ld_o, kmod_o, v_o, kk_o, kka_o, g_o) = refs
    k = k_ref[...]
    y = -(d0_ref[...] + _dot_bf16(twd_ref, wd2_ref))
    softplus = jnp.maximum(y, 0.) + jnp.log(1. + jnp.exp(-jnp.abs(y)))
    ld_o[...] = -jnp.exp(-softplus - 0.5)
    a = jax.nn.sigmoid(a0_ref[...] + _dot_bf16(ad_ref, wa2_ref))
    g_o[...] = _dot_bf16(sgd_ref, wg2_ref)
    v = v_ref[...]
    if not first_layer:
        v = v + (vf_ref[...] - v) * jax.nn.sigmoid(v0_ref[...] + _dot_bf16(vd_ref, wv2_ref))
    v_o[...] = v
    kk = k * kk_w_ref[...]
    tm, wc = kk.shape
    nblk = wc // PAIR
    sq = jnp.concatenate([(kk * kk)[:, q * PAIR:(q + 1) * PAIR] for q in range(nblk)], axis=0)
    ss = _seg_sum(sq, _ones_blockdiag())
    ss = jnp.concatenate([ss[q * tm:(q + 1) * tm] for q in range(nblk)], axis=1)
    kk = kk / jnp.maximum(jnp.sqrt(ss), 1e-12)
    kk_o[...] = kk
    kka_o[...] = kk * a
    kmod_o[...] = k * (1 + (a - 1) * ka_ref[...])


def _prep(z, acts, v_first, weights, vecs, layer, *, tm, wc, k_off, v_off):
    n = z.shape[0]
    twd, ad, sgd, vd = acts
    w_d2, w_a2, w_g2, w_v2 = weights
    d0, a0, kk_w, ka, v0 = vecs
    rw = w_d2.shape[-1]
    first_layer = v_first is None
    col = lambda off: pl.BlockSpec((tm, wc), lambda i, j: (i, j + off))
    act = lambda a_: pl.BlockSpec((tm, a_.shape[1]), lambda i, j: (i, 0))
    lw = lambda w_, l_: pl.BlockSpec((None, w_.shape[1], wc), lambda i, j: (l_, 0, j))
    vec = lambda l_: pl.BlockSpec((None, 1, wc), lambda i, j: (l_, 0, j))
    args = [z, z, twd, ad, sgd, w_d2, w_a2, w_g2, d0, a0, kk_w, ka]
    specs = [col(k_off), col(v_off), act(twd), act(ad), act(sgd), lw(w_d2, layer), lw(w_a2, layer),
             lw(w_g2, layer), vec(layer), vec(layer), vec(layer), vec(layer)]
    if not first_layer:
        args += [v_first, vd, w_v2, v0]
        specs += [col(0), act(vd), lw(w_v2, layer - 1), vec(layer - 1)]
    out = jax.ShapeDtypeStruct((n, rw), F32)
    return pl.pallas_call(
        functools.partial(_prep_kernel, first_layer=first_layer),
        grid=(n // tm, rw // wc),
        in_specs=specs,
        out_specs=[col(0)] * 6,
        out_shape=[out] * 6,
        compiler_params=pltpu.CompilerParams(dimension_semantics=("parallel", "arbitrary")),
        name="rwkv_prep",
    )(*args)


def _wkv_kernel(r_ref, d_ref, k_ref, v_ref, kk_ref, kka_ref, s0_ref, o_ref, s_ref, *, tc, npairs):
    @pl.when(pl.program_id(1) == 0)
    def _():
        s_ref[...] = s0_ref[...]

    lane = lax.broadcasted_iota(jnp.int32, (HEAD, PAIR), 1)
    row = lax.broadcasted_iota(jnp.int32, (HEAD, PAIR), 0)
    diag = ((lane & (HEAD - 1)) == row).astype(F32)
    jr = lax.broadcasted_iota(jnp.int32, (PAIR, PAIR), 0) >= HEAD
    jc = lax.broadcasted_iota(jnp.int32, (PAIR, PAIR), 1) >= HEAD
    ones_blockdiag = (jr == jc).astype(BF16)

    rows = min(tc, SUBLANES)
    row_id = lax.broadcasted_iota(jnp.int32, (rows, PAIR), 0)

    def row_group(gi, carry):
        base = pl.multiple_of(gi * rows, rows)

        def rowvec(ref, p, i):
            return ref[0, pl.ds(base, rows), p * PAIR:(p + 1) * PAIR][i:i + 1]

        o_tiles = [jnp.zeros((rows, PAIR), F32)] * npairs
        for i in range(rows):
            sk = jnp.concatenate([s_ref[0, p] * rowvec(kk_ref, p, i) for p in range(npairs)], axis=0)
            vd = jnp.concatenate([diag * rowvec(v_ref, p, i) for p in range(npairs)], axis=0)
            y_kk = jnp.dot(sk.astype(BF16), ones_blockdiag, preferred_element_type=F32)
            y_v = _seg_sum(vd, ones_blockdiag)
            qs = []
            for p in range(npairs):
                s_kk = y_kk[p * HEAD:(p + 1) * HEAD]
                v_col = y_v[p * HEAD:(p + 1) * HEAD]
                s = (s_ref[0, p] * rowvec(d_ref, p, i) - s_kk * rowvec(kka_ref, p, i)
                     + v_col * rowvec(k_ref, p, i))
                s_ref[0, p] = s
                qs.append(s * rowvec(r_ref, p, i))
            z = jnp.dot(jnp.concatenate(qs, axis=0).astype(BF16), ones_blockdiag,
                        preferred_element_type=F32)
            for p in range(npairs):
                o_col = z[p * HEAD:(p + 1) * HEAD]
                o_row = jnp.sum(o_col * diag, axis=0, keepdims=True)
                o_tiles[p] = jnp.where(row_id == i, o_row, o_tiles[p])
        for p in range(npairs):
            o_ref[0, pl.ds(base, rows), p * PAIR:(p + 1) * PAIR] = o_tiles[p]
        return carry

    lax.fori_loop(0, tc // rows, row_group, 0)


def _pair_state(s):
    b, h = s.shape[:2]
    return s.reshape(b, h // 2, 2, HEAD, HEAD).transpose(0, 1, 3, 2, 4).reshape(b, h // 2, HEAD, PAIR)


def _unpair_state(s):
    b, hp = s.shape[:2]
    return s.reshape(b, hp, HEAD, 2, HEAD).transpose(0, 1, 3, 2, 4).reshape(b, 2 * hp, HEAD, HEAD)


def _wkv(r, decay, k, v, kk, kka, s0, *, tc):
    b, t, w = r.shape
    npairs = w // PAIR
    assert t % tc == 0
    seq = pl.BlockSpec((1, tc, w), lambda i, c: (i, c, 0))
    st = pl.BlockSpec((1, npairs, HEAD, PAIR), lambda i, c: (i, 0, 0, 0))
    o, s = pl.pallas_call(
        functools.partial(_wkv_kernel, tc=tc, npairs=npairs),
        grid=(b, t // tc),
        in_specs=[seq] * 6 + [st],
        out_specs=[seq, st],
        out_shape=[jax.ShapeDtypeStruct((b, t, w), F32),
                   jax.ShapeDtypeStruct((b, npairs, HEAD, PAIR), F32)],
        compiler_params=pltpu.CompilerParams(dimension_semantics=("parallel", "arbitrary")),
        name="wkv_steps",
    )(r, decay, k, v, kk, kka, _pair_state(s0))
    return o, _unpair_state(s)


def _bdot(x, y):
    return jnp.dot(x.astype(BF16), y.astype(BF16), preferred_element_type=F32)


def _bdot_nt(x, y):
    return lax.dot_general(x.astype(BF16), y.astype(BF16), (((1,), (1,)), ((), ())),
                           preferred_element_type=F32)


def _split3(x):
    h1 = x.astype(BF16)
    r1 = x - h1.astype(F32)
    h2 = r1.astype(BF16)
    return h1, h2, (r1 - h2.astype(F32)).astype(BF16)


def _wkv_chunk_kernel(r_ref, ld_ref, k_ref, v_ref, kk_ref, kka_ref, g_ref, rk_ref, lnw_ref, lnb_ref,
                      s0_ref, o_ref, s_ref, *, npairs):
    c = HEAD

    @pl.when(pl.program_id(2) == 0)
    def _():
        s_ref[...] = s0_ref[...]

    row = lax.broadcasted_iota(jnp.int32, (c, PAIR), 0)
    col = lax.broadcasted_iota(jnp.int32, (c, PAIR), 1)
    left = col < c
    strict = (col & (c - 1)) < row
    incl = (col & (c - 1)) <= row
    tri = (lax.broadcasted_iota(jnp.int32, (c, c), 1)
           <= lax.broadcasted_iota(jnp.int32, (c, c), 0)).astype(BF16)
    same_head = ((lax.broadcasted_iota(jnp.int32, (PAIR, PAIR), 0) >= HEAD)
                 == (lax.broadcasted_iota(jnp.int32, (PAIR, PAIR), 1) >= HEAD))
    ones_blockdiag = same_head.astype(BF16)

    def halves(x):
        return jnp.concatenate([jnp.where(left, x, 0.), jnp.where(left, 0., x)], axis=1)

    def pairs(x):
        return jnp.stack([x[:, p * PAIR:(p + 1) * PAIR] for p in range(npairs)])

    def bmm(x, y):
        return lax.dot_general(x.astype(BF16), y.astype(BF16), (((2,), (1,)), ((0,), (0,))),
                               preferred_element_type=F32)

    def bmm_nt(x, y):
        return lax.dot_general(x.astype(BF16), y.astype(BF16), (((2,), (2,)), ((0,), (0,))),
                               preferred_element_type=F32)

    def seg_sum(x):
        return jnp.dot(x.reshape(npairs * c, PAIR).astype(BF16), ones_blockdiag,
                       preferred_element_type=F32).reshape(npairs, c, PAIR)

    ld = ld_ref[0]
    r, k, v = r_ref[0], k_ref[0], v_ref[0]
    b = sum(jnp.dot(tri, part, preferred_element_type=F32) for part in _split3(ld))
    eb = jnp.exp(b)
    e_inv = jnp.exp(-b)
    a_t = pairs(-kk_ref[0] * jnp.exp(b - ld))
    r_t = pairs(r * eb)
    k_t = pairs(k * e_inv)
    b_t = pairs(kka_ref[0] * e_inv)
    eb_last = pairs(eb[c - 1:c])
    v_p = pairs(v)
    s = s_ref[0]

    gram = bmm_nt(jnp.concatenate([a_t, r_t], axis=1),
                  jnp.concatenate([halves(b_t), halves(k_t)], axis=1))
    l_ab = jnp.where(strict, gram[:, :c, :PAIR], 0.)
    l_ak = jnp.where(strict, gram[:, :c, PAIR:], 0.)
    m_rb = jnp.where(incl, gram[:, c:, :PAIR], 0.)
    m_rk = jnp.where(incl, gram[:, c:, PAIR:], 0.)

    v_h = halves(v_p)
    u = bmm_nt(a_t, s) + bmm(l_ak, v_h)
    l_pow = l_ab
    n = 1
    while True:
        u = u + bmm(l_pow, halves(u))
        n *= 2
        if n >= c:
            break
        l_pow = bmm(l_pow, halves(l_pow))
    o = bmm_nt(r_t, s) + bmm(jnp.concatenate([m_rk, m_rb], axis=2),
                             jnp.concatenate([v_h, halves(u)], axis=1))
    vu = jnp.concatenate([v_p, u], axis=1)
    vu_t = jnp.stack([vu[p].T for p in range(npairs)])
    s_new = bmm(vu_t, jnp.concatenate([k_t, b_t], axis=1))
    s_ref[0] = (s + jnp.where(same_head, s_new, 0.)) * eb_last

    mean = seg_sum(o) * (1.0 / HEAD)
    dev = o - mean
    var = seg_sum(dev * dev) * (1.0 / HEAD)
    bonus = seg_sum(pairs(r * k * rk_ref[...])) * v_p
    o = dev * lax.rsqrt(var + GN_EPS) * pairs(lnw_ref[...]) + pairs(lnb_ref[...]) + bonus
    o = o * pairs(g_ref[0])
    for p in range(npairs):
        o_ref[0, :, p * PAIR:(p + 1) * PAIR] = o[p].astype(o_ref.dtype)


def _blockdiag_state(s):
    b, h = s.shape[:2]
    s = s.reshape(b, h // 2, 2, HEAD, HEAD)
    z = jnp.zeros_like(s[:, :, 0])
    return jnp.concatenate([jnp.concatenate([s[:, :, 0], z], axis=-1),
                            jnp.concatenate([z, s[:, :, 1]], axis=-1)], axis=-2)


def _unblockdiag_state(s):
    b, hp = s.shape[:2]
    return jnp.stack([s[:, :, :HEAD, :HEAD], s[:, :, HEAD:, HEAD:]], axis=2).reshape(
        b, 2 * hp, HEAD, HEAD)


def _wkv_chunked(r, log_decay, k, v, kk, kka, g, r_k, ln_w, ln_b, s0, *, npairs, r_off=0):
    b, t, w = log_decay.shape
    wb = npairs * PAIR
    assert t % HEAD == 0 and (w // PAIR) % npairs == 0 and r_off % wb == 0
    seq = pl.BlockSpec((1, HEAD, wb), lambda i, j, c: (i, c, j))
    r_seq = pl.BlockSpec((1, HEAD, wb), lambda i, j, c: (i, c, j + r_off // wb))
    vec = pl.BlockSpec((1, wb), lambda i, j, c: (0, j))
    st = pl.BlockSpec((1, npairs, PAIR, PAIR), lambda i, j, c: (i, j, 0, 0))
    o, s = pl.pallas_call(
        functools.partial(_wkv_chunk_kernel, npairs=npairs),
        grid=(b, w // wb, t // HEAD),
        in_specs=[r_seq] + [seq] * 6 + [vec] * 3 + [st],
        out_specs=[seq, st],
        out_shape=[jax.ShapeDtypeStruct((b, t, w), BF16),
                   jax.ShapeDtypeStruct((b, w // PAIR, PAIR, PAIR), F32)],
        compiler_params=pltpu.CompilerParams(
            dimension_semantics=("parallel", "parallel", "arbitrary")),
        name="wkv_chunked",
    )(r, log_decay, k, v, kk, kka, g, r_k.reshape(1, w), ln_w.reshape(1, w), ln_b.reshape(1, w),
      _blockdiag_state(s0))
    return o, _unblockdiag_state(s)


def _moe_up_kernel(te_ref, tv_ref, x_ref, wg_ref, wu_ref, h_ref):
    i = pl.program_id(0)

    @pl.when(tv_ref[i] > 0)
    def _():
        x = x_ref[...]
        g = jnp.dot(x, wg_ref[...].astype(BF16), preferred_element_type=F32)
        u = jnp.dot(x, wu_ref[...].astype(BF16), preferred_element_type=F32)
        h_ref[...] = (g * jax.nn.sigmoid(g) * u).astype(h_ref.dtype)

    @pl.when(tv_ref[i] == 0)
    def _():
        h_ref[...] = jnp.zeros_like(h_ref)


def _moe_down_kernel(te_ref, tv_ref, h_ref, wd_ref, y_ref):
    i = pl.program_id(0)

    @pl.when(tv_ref[i] > 0)
    def _():
        y_ref[...] = jnp.dot(h_ref[...], wd_ref[...].astype(BF16), preferred_element_type=F32)

    @pl.when(tv_ref[i] == 0)
    def _():
        y_ref[...] = jnp.zeros_like(y_ref)


def _moe_experts(xs, tile_expert, tile_valid, w_gate, w_up, w_down, layer):
    p, d = xs.shape
    de = w_gate.shape[-1]
    nt = p // MOE_TM
    hid = pl.pallas_call(
        _moe_up_kernel,
        grid_spec=pltpu.PrefetchScalarGridSpec(
            num_scalar_prefetch=2, grid=(nt,),
            in_specs=[pl.BlockSpec((MOE_TM, d), lambda i, te, tv: (i, 0)),
                      pl.BlockSpec((None, None, d, de), lambda i, te, tv: (layer, te[i], 0, 0)),
                      pl.BlockSpec((None, None, d, de), lambda i, te, tv: (layer, te[i], 0, 0))],
            out_specs=pl.BlockSpec((MOE_TM, de), lambda i, te, tv: (i, 0))),
        out_shape=jax.ShapeDtypeStruct((p, de), BF16),
        compiler_params=pltpu.CompilerParams(dimension_semantics=("arbitrary",)),
        name="moe_up",
    )(tile_expert, tile_valid, xs, w_gate, w_up)
    return pl.pallas_call(
        _moe_down_kernel,
        grid_spec=pltpu.PrefetchScalarGridSpec(
            num_scalar_prefetch=2, grid=(nt,),
            in_specs=[pl.BlockSpec((MOE_TM, de), lambda i, te, tv: (i, 0)),
                      pl.BlockSpec((None, None, de, d), lambda i, te, tv: (layer, te[i], 0, 0))],
            out_specs=pl.BlockSpec((MOE_TM, d), lambda i, te, tv: (i, 0))),
        out_shape=jax.ShapeDtypeStruct((p, d), F32),
        compiler_params=pltpu.CompilerParams(dimension_semantics=("arbitrary",)),
        name="moe_down",
    )(tile_expert, tile_valid, hid, w_down)


def _hier_moe(h2, layer, w_route, b_route_group, b_route_expert, w_gate, w_up, w_down, *, tm_route):
    n, d = h2.shape
    n_groups = b_route_group.shape[-1]
    n_experts = b_route_expert.shape[-1]
    logits = _mm(h2, w_route, tm=tm_route, tn=w_route.shape[-1], b_lead=(layer,))
    p_group = jax.nn.softmax(logits[:, :n_groups] + b_route_group[layer], axis=-1)
    g_idx = jnp.argmax(p_group, axis=-1, keepdims=True)
    p_top = jnp.max(p_group, axis=-1, keepdims=True)
    le = (logits[:, n_groups:n_groups + n_experts] + b_route_expert[layer])
    le = le.reshape(n, n_groups, EXP_PER_GROUP)
    le = jnp.take_along_axis(le, g_idx[:, :, None], axis=1)[:, 0]
    v_top, e_idx, rest = [], [], le
    for _ in range(TOP_K):
        best = jnp.argmax(rest, axis=-1, keepdims=True)
        v_top.append(jnp.max(rest, axis=-1, keepdims=True))
        e_idx.append(best)
        rest = jnp.where(jnp.arange(EXP_PER_GROUP) == best, -jnp.inf, rest)
    v_top, e_idx = jnp.concatenate(v_top, axis=-1), jnp.concatenate(e_idx, axis=-1)
    wts = p_top * jax.nn.softmax(v_top, axis=-1)
    gidx = (g_idx * EXP_PER_GROUP + e_idx).astype(jnp.int32)

    na = n * TOP_K
    nt = na // MOE_TM + n_experts
    e_flat = gidx.reshape(na)
    order = jnp.argsort(e_flat, stable=True).astype(jnp.int32)
    sorted_e = e_flat[order]
    counts = jnp.zeros((n_experts,), jnp.int32).at[e_flat].add(1)
    tiles_per = (counts + MOE_TM - 1) // MOE_TM
    tile_end = jnp.cumsum(tiles_per)
    pad_start = (tile_end - tiles_per) * MOE_TM
    count_start = jnp.cumsum(counts) - counts
    dest = pad_start[sorted_e] + jnp.arange(na, dtype=jnp.int32) - count_start[sorted_e]
    row_token = jnp.zeros((nt * MOE_TM,), jnp.int32).at[dest].set(order // TOP_K)
    pos = jnp.zeros((na,), jnp.int32).at[order].set(dest)
    tile_ids = jnp.arange(nt, dtype=jnp.int32)
    tile_expert = jnp.minimum(jnp.searchsorted(tile_end, tile_ids, side='right'),
                              n_experts - 1).astype(jnp.int32)
    tile_valid = (tile_ids < tile_end[-1]).astype(jnp.int32)

    xs = jnp.take(h2, row_token, axis=0, mode="clip")
    ys = _moe_experts(xs, tile_expert, tile_valid, w_gate, w_up, w_down, layer)
    return ys, pos, wts


def _rms_norm(x, g):
    return x * lax.rsqrt(jnp.mean(x * x, axis=-1, keepdims=True) + RMS_EPS) * g


def _pool_diffs(u, u_past, start_pos):
    b, t, pw = u.shape
    grp = pw // len(POOL_WINDOWS)
    ext = jnp.concatenate([u_past, u], axis=1)
    pos = start_pos + jnp.arange(t)
    diffs = []
    for gi, win in enumerate(POOL_WINDOWS):
        e = ext[:, :, gi * grp:(gi + 1) * grp]
        win_sum = e[:, POOL_BUF:POOL_BUF + t]
        for i in range(1, win):
            win_sum = win_sum + e[:, POOL_BUF - i:POOL_BUF - i + t]
        cnt = jnp.minimum(win, pos + 1).astype(F32)[None, :, None]
        diffs.append(win_sum / cnt - e[:, POOL_BUF:])
    return jnp.concatenate(diffs, axis=-1).reshape(b * t, pw), ext[:, t:]


def kernel(x_prompt, x_sample, state_shift, state_pool, state_wkv, c_prompt, c_sample, w_ada, b_ada, g_mix, g_ffn, g_final, w_in, w_vres_down, mu, mu_vres, w_pool, pool_scale, w_decay0, w_decay2, w_a0, w_a2, w_g2, w_v0, w_v2, k_k, k_a, r_k, ln_x_w, ln_x_b, w_out, w_route_group, b_route_group, w_route_expert, b_route_expert, w_exp_gate, w_exp_up, w_exp_down):
    bp, tp, d = x_prompt.shape
    bs, ts, _ = x_sample.shape
    depth = w_in.shape[0]
    n_heads = state_wkv.shape[2]
    rw = n_heads * HEAD
    pw = state_pool.shape[-1]
    n_mod = w_ada.shape[1] // d
    n_p, n_s = bp * tp, bs * ts
    n_tok = n_p + n_s
    c_main = pw + 3 * rw
    n_tail = w_in.shape[-1] - c_main
    v_lora = w_vres_down.shape[-1]
    w_lora, a_lora = w_decay2.shape[1], w_a2.shape[1]

    tm_tok = n_tok // 8
    tm_p = tp // 2
    tm_norm = math.gcd(256, tp, n_s)
    tm_prep = tm_norm
    tm_mix = math.gcd(512, tp, n_s)
    npairs = max(q for q in range(1, WKV_PAIRS + 1)
                 if (n_heads // 2) % q == 0 and pw % (q * PAIR) == 0)

    def stream(a_p, a_s):
        return jnp.concatenate([a_p.reshape(n_p, -1), a_s.reshape(n_s, -1)], axis=0)

    c_all = jnp.concatenate([c_prompt, c_sample], axis=0)
    rows_c = -(-c_all.shape[0] // 16) * 16
    c_act = jnp.pad(jax.nn.silu(c_all), ((0, rows_c - c_all.shape[0]), (0, 0))).astype(BF16)
    mod = _mm(c_act, w_ada, tm=rows_c, tn=512)[:bp + bs].reshape(bp + bs, n_mod, d)

    zeros_l = lambda *shape: jnp.zeros((1,) + shape, F32)
    w_tail = jnp.concatenate(
        [w_in[:, :, c_main:], jnp.concatenate([zeros_l(d, v_lora), w_vres_down], axis=0)], axis=-1)
    mu_main = jnp.concatenate([jnp.zeros((depth, pw), F32), mu[:, :3 * rw]], axis=-1)[:, None]
    mu_tail = jnp.concatenate(
        [mu[:, 3 * rw:], jnp.concatenate([zeros_l(v_lora), mu_vres], axis=0)], axis=-1)[:, None]
    route_pad = 128 - (w_route_group.shape[-1] + w_route_expert.shape[-1])
    w_route = jnp.concatenate(
        [w_route_group, w_route_expert, jnp.zeros((depth, d, route_pad), F32)], axis=-1)
    pool_scale3 = pool_scale[:, None]

    lora_weights = (w_decay2, w_a2, w_g2, w_v2)
    lora_vecs = (w_decay0[:, None], w_a0[:, None], k_k[:, None], k_a[:, None], w_v0[:, None])
    assert rw % pw == 0 and pw % (npairs * PAIR) == 0

    def mixer_inputs(z, zt, v_first, l, tm):
        acts = (jnp.tanh(zt[:, :w_lora]).astype(BF16),
                zt[:, w_lora:w_lora + a_lora].astype(BF16),
                jax.nn.sigmoid(zt[:, w_lora + a_lora:n_tail]).astype(BF16),
                zt[:, n_tail:].astype(BF16))
        return _prep(z, acts, v_first, lora_weights, lora_vecs, l, tm=tm, wc=pw,
                     k_off=1 + rw // pw, v_off=1 + 2 * (rw // pw))

    x = stream(x_prompt, x_sample)
    vf_p = vf_s = None
    shift_p, pool_p, wkv_p, shift_s, pool_s, wkv_s = [], [], [], [], [], []
    zero_state = jnp.zeros((bp, n_heads, HEAD, HEAD), F32)
    for l in range(depth):
        m = mod + b_ada[l]
        m_seq = m[:bp].transpose(1, 0, 2)[:, :, None]
        m_tok = jnp.repeat(m[bp:], ts, axis=0).transpose(1, 0, 2)
        x_last = jnp.concatenate([x[tp - 1:n_p:tp], x[n_p + ts - 1::ts]])
        h_last = _rms_norm(x_last, g_mix[l]) * (1 + m[:, 1]) + m[:, 0]
        shift_p.append(h_last[:bp])
        shift_s.append(h_last[bp:])
        rows256 = _RowMod(m_seq, m_tok, tp, tm_norm)
        rows512 = _RowMod(m_seq, m_tok, tp, tm_mix)
        h_bf = _norm_mod(x, g_mix[l], rows256, 1, 0)
        hs_bf = jnp.concatenate([state_shift[l][:, None].astype(BF16), h_bf[n_p:].reshape(bs, ts, d)],
                                axis=1).reshape(bs * (ts + 1), d)

        z_p = _mm_lerp(h_bf, w_in, mu_main, l, tm=tm_p, tn=512, m=n_p, n=c_main,
                       tiles_per_seq=tp // tm_p)
        zt_p = _mm_lerp(h_bf, w_tail, mu_tail, l, tm=tm_p, tn=n_tail + v_lora, m=n_p,
                        n=n_tail + v_lora, tiles_per_seq=tp // tm_p)
        drop_first = lambda a_: a_.reshape(bs, ts + 1, -1)[:, 1:].reshape(n_s, -1)
        z_s = drop_first(_mm_lerp(hs_bf, w_in, mu_main, l, tm=hs_bf.shape[0], tn=512,
                                  m=hs_bf.shape[0], n=c_main, tiles_per_seq=None))
        zt_s = drop_first(_mm_lerp(hs_bf, w_tail, mu_tail, l, tm=hs_bf.shape[0],
                                   tn=n_tail + v_lora, m=hs_bf.shape[0], n=n_tail + v_lora,
                                   tiles_per_seq=None))

        d_p, np_p = _pool_diffs(z_p[:, :pw].reshape(bp, tp, pw), jnp.zeros((bp, POOL_BUF, pw), F32), 0)
        d_s, np_s = _pool_diffs(z_s[:, :pw].reshape(bs, ts, pw), state_pool[l], PAST_LEN)
        pool_p.append(np_p)
        pool_s.append(np_s)
        pool_out = _pool_mm(jnp.concatenate([d_p, d_s], axis=0).astype(BF16), w_pool, pool_scale3,
                            l, tm=tm_tok)

        in_p = mixer_inputs(z_p, zt_p, vf_p, l, tm_prep)
        in_s = mixer_inputs(z_s, zt_s, vf_s, l, tm_prep)
        if l == 0:
            vf_p, vf_s = in_p[2], in_s[2]
        seq_p = lambda a_: a_.reshape(bp, tp, -1)
        seq_s = lambda a_: a_.reshape(bs, ts, rw)
        o_p, s_p = _wkv_chunked(seq_p(z_p), *[seq_p(a_) for a_ in in_p], r_k[l], ln_x_w[l], ln_x_b[l],
                                zero_state, npairs=npairs, r_off=pw)
        ld_s, k_s, v_s, kk_s, kka_s, g_s = in_s
        r_s = z_s[:, pw:pw + rw]
        o_s, s_s = _wkv(seq_s(r_s), seq_s(jnp.exp(ld_s)), seq_s(k_s), seq_s(v_s), seq_s(kk_s),
                        seq_s(kka_s), state_wkv[l], tc=ts)
        wkv_p.append(s_p)
        wkv_s.append(s_s)
        heads = lambda a_: a_.reshape(n_s, n_heads, HEAD)
        o_s = heads(o_s)
        mean = jnp.mean(o_s, axis=-1, keepdims=True)
        var = jnp.mean(jnp.square(o_s - mean), axis=-1, keepdims=True)
        o_s = ((o_s - mean) * lax.rsqrt(var + GN_EPS)).reshape(n_s, rw) * ln_x_w[l] + ln_x_b[l]
        bonus = jnp.sum(heads(r_s) * heads(k_s) * r_k[l], axis=-1, keepdims=True) * heads(v_s)
        o_s = ((o_s + bonus.reshape(n_s, rw)) * g_s).astype(BF16)
        o = jnp.concatenate([o_p.reshape(n_p, rw), o_s], axis=0)
        x = _mix_out(pool_out, o, w_out, x, rows512, 2, l, tn=512)

        h2 = _norm_mod(x, g_ffn[l], rows256, 4, 3)
        ys, pos, wts = _hier_moe(h2, l, w_route, b_route_group, b_route_expert,
                                 w_exp_gate, w_exp_up, w_exp_down, tm_route=tm_tok)
        x = _moe_combine(x, ys, pos, wts, rows256, 5)

    y = _rms_norm(x, g_final)
    return (y[:n_p].reshape(bp, tp, d), y[n_p:].reshape(bs, ts, d),
            jnp.stack(shift_p), jnp.stack(pool_p), jnp.stack(wkv_p),
            jnp.stack(shift_s), jnp.stack(pool_s), jnp.stack(wkv_s))
```
